```python
import math
import jax, jax.numpy as jnp
from jax import lax
import numpy as np

D_MODEL = 1024
BATCH = 2
SEQ = 8192
DEPTH = 2

SB_HEADS = 8
SB_HEAD_DIM = 64
SB_WIDTH = SB_HEADS * SB_HEAD_DIM
SB_BLOCK = 128
HG_HEADS = 4
HG_DK = 128
HG_DV = 128
HG_WIDTH_K = HG_HEADS * HG_DK
HG_WIDTH_V = HG_HEADS * HG_DV
HG_CHUNK = 64
HG_F_FLOOR = 1e-30
SSD_HEADS = 8
SSD_HEAD_DIM = 64
SSD_WIDTH = SSD_HEADS * SSD_HEAD_DIM
SSD_GROUPS = 2
SSD_STATE = 64
SSD_CONV = 4
SSD_CHUNK = 128
SSD_CONV_DIM = SSD_WIDTH + 2 * SSD_GROUPS * SSD_STATE
N_BRANCHES = 3
IN_SIZES = (SB_WIDTH, SB_WIDTH, SB_WIDTH,
            HG_WIDTH_K, HG_WIDTH_K, HG_WIDTH_V, HG_WIDTH_V,
            SSD_WIDTH, SSD_CONV_DIM, SSD_HEADS,
            D_MODEL, D_MODEL, D_MODEL)
IN_COLS = 3 * SB_WIDTH + 2 * HG_WIDTH_K + 2 * HG_WIDTH_V + SSD_WIDTH + SSD_CONV_DIM + SSD_HEADS + N_BRANCHES * D_MODEL
PEER_HEADS = 8
PEER_NKEYS = 128
PEER_EXPERTS = PEER_NKEYS * PEER_NKEYS
PEER_QDIM = 256
PEER_TOPK = 16
PEER_TOKEN_BLOCK = 128
EPS = 1e-6
NEG_BIG = -1e30

kernel_name = 'sb_hgrn2_ssd_peer_hybrid'


def rmsnorm(x, w):
    xf = x.astype(jnp.float32)
    y = xf * lax.rsqrt(jnp.mean(xf * xf, axis=-1, keepdims=True) + EPS)
    return (y * w.astype(jnp.float32)).astype(x.dtype)


def stick_breaking_attention(q, k, v):
    Bsz, S, H, Dh = q.shape
    nb = S // SB_BLOCK
    scale = Dh ** -0.5
    kf = k.astype(jnp.float32)
    vf = v.astype(jnp.float32)
    qb = q.astype(jnp.float32).reshape(Bsz, nb, SB_BLOCK, H, Dh).transpose(1, 0, 2, 3, 4)
    key_pos = jnp.arange(S)

    def block(args):
        q_blk, blk = args
        q_pos = blk * SB_BLOCK + jnp.arange(SB_BLOCK)
        z = jnp.einsum('bqhd,bkhd->bhqk', q_blk, kf) * scale
        before = key_pos[None, :] < q_pos[:, None]
        log_beta = jax.nn.log_sigmoid(z)
        log_rest = jnp.where(before, jax.nn.log_sigmoid(-z), 0.0)
        tail = lax.cumsum(log_rest, axis=3, reverse=True) - log_rest
        a = jnp.exp(jnp.where(before, log_beta + tail, NEG_BIG))
        return jnp.einsum('bhqk,bkhd->bqhd', a, vf)

    out = lax.map(block, (qb, jnp.arange(nb)))
    return out.transpose(1, 0, 2, 3, 4).reshape(Bsz, S, H * Dh).astype(q.dtype)


def hgrn2_mixer(q_raw, f_raw, i_raw, g_raw, lb, norm_w):
    Bsz, S, _ = q_raw.shape
    nc = S // HG_CHUNK
    f32 = jnp.float32
    lb = lb.astype(f32).reshape(HG_HEADS, HG_DK)
    f_logit = f_raw.astype(f32).reshape(Bsz, S, HG_HEADS, HG_DK)
    f = lb + (1.0 - lb) * jax.nn.sigmoid(f_logit)
    log_f = jnp.log(jnp.maximum(f, HG_F_FLOOR))
    key = (1.0 - lb) * jax.nn.sigmoid(-f_logit)
    q = jax.nn.silu(q_raw.astype(f32)).reshape(Bsz, S, HG_HEADS, HG_DK) * (HG_DK ** -0.5)
    val = i_raw.astype(f32).reshape(Bsz, S, HG_HEADS, HG_DV)

    def chunks(t):
        return t.reshape(Bsz, nc, HG_CHUNK, HG_HEADS, t.shape[-1]).transpose(1, 0, 3, 2, 4)

    causal = jnp.tril(jnp.ones((HG_CHUNK, HG_CHUNK), dtype=bool))

    def step(state, inp):
        qc, kc, vc, gc = inp
        b = jnp.cumsum(gc, axis=2)
        diff = b[:, :, :, None, :] - b[:, :, None, :, :]
        decay = jnp.exp(jnp.where(causal[:, :, None], diff, NEG_BIG))
        scores = jnp.einsum('bhtk,bhtsk,bhsk->bhts', qc, decay, kc)
        o = jnp.einsum('bhts,bhsv->bhtv', scores, vc) + jnp.einsum('bhtk,bhkv->bhtv', qc * jnp.exp(b), state)
        b_last = b[:, :, -1, :]
        new_state = jnp.exp(b_last)[..., None] * state + jnp.einsum('bhsk,bhsv->bhkv', kc * jnp.exp(b_last[:, :, None, :] - b), vc)
        return new_state, o

    state0 = jnp.zeros((Bsz, HG_HEADS, HG_DK, HG_DV), f32)
    _, o = lax.scan(step, state0, (chunks(q), chunks(key), chunks(val), chunks(log_f)))
    o = o.transpose(1, 0, 3, 2, 4).reshape(Bsz, S, HG_HEADS, HG_DV)
    gate = jax.nn.silu(g_raw.astype(f32)).reshape(Bsz, S, HG_HEADS, HG_DV)
    o = rmsnorm(o, norm_w) * gate
    return o.reshape(Bsz, S, HG_WIDTH_V).astype(q_raw.dtype)


def causal_depthwise_conv(x, w, b):
    K, C = w.shape
    y = lax.conv_general_dilated(x, w[:, None, :].astype(x.dtype), window_strides=(1,),
                                 padding=[(K - 1, 0)], dimension_numbers=('NWC', 'WIO', 'NWC'),
                                 feature_group_count=C)
    return y + b.astype(x.dtype)


def segsum(a):
    T = a.shape[-1]
    rep = jnp.broadcast_to(a[..., :, None], a.shape + (T,))
    rep = jnp.where(jnp.tril(jnp.ones((T, T), dtype=bool), -1), rep, 0.0)
    ss = jnp.cumsum(rep, axis=-2)
    return jnp.where(jnp.tril(jnp.ones((T, T), dtype=bool)), ss, NEG_BIG)


def ssd_chunked(x, a, b_h, c_h):
    Bsz, S, H, P = x.shape
    N = b_h.shape[-1]
    L = SSD_CHUNK
    nc = S // L
    x = x.reshape(Bsz, nc, L, H, P)
    b_h = b_h.reshape(Bsz, nc, L, H, N)
    c_h = c_h.reshape(Bsz, nc, L, H, N)
    a = a.reshape(Bsz, nc, L, H).transpose(0, 3, 1, 2)
    a_cum = jnp.cumsum(a, axis=-1)
    seg = jnp.exp(segsum(a))
    scores = jnp.einsum('bclhn,bcshn->bhcls', c_h, b_h) * seg
    y_diag = jnp.einsum('bhcls,bcshp->bclhp', scores, x)
    decay_states = jnp.exp(a_cum[..., -1:] - a_cum)
    states = jnp.einsum('bclhn,bhcl,bclhp->bchpn', b_h, decay_states, x)
    states = jnp.concatenate([jnp.zeros_like(states[:, :1]), states], axis=1)
    chunk_tot = jnp.pad(a_cum[..., -1], ((0, 0), (0, 0), (1, 0)))
    decay_chunk = jnp.exp(segsum(chunk_tot))
    states = jnp.einsum('bhzc,bchpn->bzhpn', decay_chunk, states)[:, :-1]
    y_off = jnp.einsum('bclhn,bchpn,bhcl->bclhp', c_h, states, jnp.exp(a_cum))
    return (y_diag + y_off).reshape(Bsz, S, H, P)


def ssd_mixer(z, xbc, dt_raw, conv_w, conv_b, dt_bias, a_log, d_skip, norm_w):
    Bsz, S, _ = z.shape
    f32 = jnp.float32
    xbc = jax.nn.silu(causal_depthwise_conv(xbc, conv_w, conv_b))
    xs, b_in, c_in = jnp.split(xbc, [SSD_WIDTH, SSD_WIDTH + SSD_GROUPS * SSD_STATE], axis=-1)
    rep = SSD_HEADS // SSD_GROUPS
    xs = xs.astype(f32).reshape(Bsz, S, SSD_HEADS, SSD_HEAD_DIM)
    b_h = jnp.repeat(b_in.astype(f32).reshape(Bsz, S, SSD_GROUPS, SSD_STATE), rep, axis=2)
    c_h = jnp.repeat(c_in.astype(f32).reshape(Bsz, S, SSD_GROUPS, SSD_STATE), rep, axis=2)
    dt = jax.nn.softplus(dt_raw.astype(f32) + dt_bias.astype(f32))
    a = -jnp.exp(a_log.astype(f32)) * dt
    y = ssd_chunked(xs * dt[..., None], a, b_h, c_h)
    y = y + d_skip.astype(f32)[:, None] * xs
    y = y.reshape(Bsz, S, SSD_WIDTH) * jax.nn.silu(z.astype(f32))
    y = rmsnorm(y.reshape(Bsz, S, SSD_GROUPS, SSD_WIDTH // SSD_GROUPS), norm_w.reshape(SSD_GROUPS, -1))
    return y.reshape(Bsz, S, SSD_WIDTH).astype(z.dtype)


def hybrid_mixer(h, lb, w_in, hg_norm_w, conv_w, conv_b, dt_bias, a_log, d_skip, ssd_norm_w,
                 w_branch_sb, w_branch_hg, w_branch_ssd, w_out):
    Bsz, S, _ = h.shape
    proj = h @ w_in
    points = [int(p) for p in np.cumsum(IN_SIZES)[:-1]]
    (sb_q, sb_k, sb_v, hg_f, hg_q, hg_i, hg_g, ssd_z, ssd_xbc, ssd_dt,
     gate_sb, gate_hg, gate_ssd) = jnp.split(proj, points, axis=-1)
    heads = lambda t: t.reshape(Bsz, S, SB_HEADS, SB_HEAD_DIM)
    y_sb = stick_breaking_attention(heads(sb_q), heads(sb_k), heads(sb_v))
    y_hg = hgrn2_mixer(hg_q, hg_f, hg_i, hg_g, lb, hg_norm_w)
    y_ssd = ssd_mixer(ssd_z, ssd_xbc, ssd_dt, conv_w, conv_b, dt_bias, a_log, d_skip, ssd_norm_w)
    merged = (jax.nn.sigmoid(gate_sb) * (y_sb @ w_branch_sb)
              + jax.nn.sigmoid(gate_hg) * (y_hg @ w_branch_hg)
              + jax.nn.sigmoid(gate_ssd) * (y_ssd @ w_branch_ssd))
    return merged @ w_out


def peer_ffn(x, w_q, sub_keys, u, v):
    Bsz, S, D = x.shape
    T = Bsz * S
    xt = x.reshape(T, D)
    q = (xt @ w_q).reshape(T, PEER_HEADS, 2, PEER_QDIM // 2)
    s = jnp.einsum('thpd,hpkd->thpk', q, sub_keys).astype(jnp.float32)
    s1, i1 = lax.top_k(s[:, :, 0], PEER_TOPK)
    s2, i2 = lax.top_k(s[:, :, 1], PEER_TOPK)
    cand_s = (s1[..., :, None] + s2[..., None, :]).reshape(T, PEER_HEADS, PEER_TOPK * PEER_TOPK)
    cand_i = (i1[..., :, None] * PEER_NKEYS + i2[..., None, :]).reshape(T, PEER_HEADS, PEER_TOPK * PEER_TOPK)
    top_s, pos = lax.top_k(cand_s, PEER_TOPK)
    expert_idx = jnp.take_along_axis(cand_i, pos, axis=-1)
    gates = jax.nn.softmax(top_s, axis=-1)
    nb = T // PEER_TOKEN_BLOCK
    HK = PEER_HEADS * PEER_TOPK

    def block(args):
        xb, idx, g = args
        ub = u[idx]
        vb = v[idx]
        act = jax.nn.gelu(jnp.einsum('td,tjd->tj', xb, ub).astype(jnp.float32), approximate=False)
        return jnp.einsum('tj,tjd->td', (g * act).astype(xb.dtype), vb)

    out = lax.map(block, (xt.reshape(nb, PEER_TOKEN_BLOCK, D),
                          expert_idx.reshape(nb, PEER_TOKEN_BLOCK, HK),
                          gates.reshape(nb, PEER_TOKEN_BLOCK, HK)))
    return out.reshape(Bsz, S, D)


def setup_inputs(seed: int = 0) -> dict:
    key = jax.random.key(seed)
    ks = jax.random.split(key, 24)
    f32 = jnp.float32
    nrm = lambda k, shape, scale: jax.random.normal(k, shape, f32) * scale
    dt = jnp.exp(jax.random.uniform(ks[7], (DEPTH, SSD_HEADS), f32, math.log(1e-3), math.log(1e-1)))
    return {
        'x': jax.random.normal(ks[0], (BATCH, SEQ, D_MODEL), f32),
        'norm1_w': 1.0 + nrm(ks[1], (DEPTH, D_MODEL), 0.02),
        'w_in': nrm(ks[2], (DEPTH, D_MODEL, IN_COLS), D_MODEL ** -0.5),
        'hg_lb_logits': nrm(ks[3], (DEPTH, HG_WIDTH_K), 1.0),
        'hg_norm_w': 1.0 + nrm(ks[4], (DEPTH, HG_DV), 0.02),
        'ssd_conv_w': nrm(ks[5], (DEPTH, SSD_CONV, SSD_CONV_DIM), SSD_CONV ** -0.5),
        'ssd_conv_b': nrm(ks[6], (DEPTH, SSD_CONV_DIM), 0.02),
        'ssd_dt_bias': dt + jnp.log(-jnp.expm1(-dt)),
        'ssd_a_log': jnp.log(jax.random.uniform(ks[8], (DEPTH, SSD_HEADS), f32, 1.0, 16.0)),
        'ssd_d': 1.0 + nrm(ks[9], (DEPTH, SSD_HEADS), 0.1),
        'ssd_norm_w': 1.0 + nrm(ks[10], (DEPTH, SSD_WIDTH), 0.02),
        'w_branch_sb': nrm(ks[11], (DEPTH, SB_WIDTH, D_MODEL), SB_WIDTH ** -0.5),
        'w_branch_hg': nrm(ks[12], (DEPTH, HG_WIDTH_V, D_MODEL), HG_WIDTH_V ** -0.5),
        'w_branch_ssd': nrm(ks[13], (DEPTH, SSD_WIDTH, D_MODEL), SSD_WIDTH ** -0.5),
        'w_out': nrm(ks[14], (DEPTH, D_MODEL, D_MODEL), D_MODEL ** -0.5),
        'norm2_w': 1.0 + nrm(ks[15], (DEPTH, D_MODEL), 0.02),
        'peer_w_q': nrm(ks[16], (DEPTH, D_MODEL, PEER_HEADS * PEER_QDIM), D_MODEL ** -0.5),
        'peer_sub_keys': nrm(ks[17], (DEPTH, PEER_HEADS, 2, PEER_NKEYS, PEER_QDIM // 2), (PEER_QDIM // 2) ** -0.5),
        'peer_u': nrm(ks[18], (DEPTH, PEER_EXPERTS, D_MODEL), D_MODEL ** -0.5),
        'peer_v': nrm(ks[19], (DEPTH, PEER_EXPERTS, D_MODEL), PEER_HEADS ** -0.5),
        'final_norm_w': 1.0 + nrm(ks[20], (D_MODEL,), 0.02),
    }


def reference(x, norm1_w, w_in, hg_lb_logits, hg_norm_w, ssd_conv_w, ssd_conv_b, ssd_dt_bias,
              ssd_a_log, ssd_d, ssd_norm_w, w_branch_sb, w_branch_hg, w_branch_ssd, w_out,
              norm2_w, peer_w_q, peer_sub_keys, peer_u, peer_v, final_norm_w):
    gamma = jax.nn.softmax(hg_lb_logits.astype(jnp.float32), axis=0)
    lower_bounds = jnp.cumsum(gamma, axis=0) - gamma[0]
    for l in range(DEPTH):
        h = rmsnorm(x, norm1_w[l])
        x = x + hybrid_mixer(h, lower_bounds[l], w_in[l], hg_norm_w[l], ssd_conv_w[l], ssd_conv_b[l],
                             ssd_dt_bias[l], ssd_a_log[l], ssd_d[l], ssd_norm_w[l],
                             w_branch_sb[l], w_branch_hg[l], w_branch_ssd[l], w_out[l])
        h = rmsnorm(x, norm2_w[l])
        x = x + peer_ffn(h, peer_w_q[l], peer_sub_keys[l], peer_u[l], peer_v[l])
    return rmsnorm(x, final_norm_w)
```

```python
import functools

import numpy as np
import jax
import jax.numpy as jnp
from jax import lax
from jax.experimental import pallas as pl
from jax.experimental.pallas import tpu as pltpu

F32 = jnp.float32
BF16 = jnp.bfloat16

D_MODEL = 1024
SB_HEAD_DIM = 64
SB_WIDTH = 512
HG_HEADS = 4
HG_DK = 128
HG_CHUNK = 64
HG_F_FLOOR = 1e-30
SSD_HEADS = 8
SSD_HEAD_DIM = 64
SSD_WIDTH = 512
SSD_STATE = 64
SSD_CONV = 4
SSD_CHUNK = 128
SSD_CONV_DIM = 768
PEER_HEADS = 8
PEER_NKEYS = 128
PEER_TOPK = 16
EPS = 1e-6
NEG_BIG = -1e30

LANES = 128
SUBLANES = 8
VMEM_LIMIT = 56 * 1024 * 1024

COL_HG = 0
COL_Z = 2048
COL_DT = 2560
COL_GATE = 3072
COL_XBC = 6144
COL_SB = 6912
IN_COLS_PADDED = 8448


def _cparams(sem):
    return pltpu.CompilerParams(dimension_semantics=sem, vmem_limit_bytes=VMEM_LIMIT)


def _dot(a, b):
    return jnp.dot(a, b, preferred_element_type=F32)


def _dot_nt(a, b):
    return lax.dot_general(a, b, (((1,), (1,)), ((), ())), preferred_element_type=F32)


def _dot_tn(a, b):
    return lax.dot_general(a, b, (((0,), (0,)), ((), ())), preferred_element_type=F32)


def _sigmoid_pair(x):
    e = jnp.exp(-jnp.abs(x))
    inv = 1.0 / (1.0 + e)
    small = e * inv
    pos = x >= 0
    return jnp.where(pos, inv, small), jnp.where(pos, small, inv)


def _silu(x):
    return x * _sigmoid_pair(x)[0]


def _split3(x, axis):
    hi = x.astype(BF16)
    r1 = x - hi.astype(F32)
    mid = r1.astype(BF16)
    lo = (r1 - mid.astype(F32)).astype(BF16)
    return jnp.concatenate([hi, mid, lo], axis=axis)


def _inproj_kernel(x_ref, nw_ref, w_ref, o_ref, h_ref):
    @pl.when(pl.program_id(1) == 0)
    def _():
        x = x_ref[...]
        ms = jnp.mean(x * x, axis=-1, keepdims=True)
        h_ref[...] = (x * lax.rsqrt(ms + EPS) * nw_ref[...]).astype(BF16)

    o_ref[...] = _dot(h_ref[...], w_ref[...])


def _inproj(x, norm_w, w, tm, tn):
    T, D = x.shape
    N = w.shape[1]
    return pl.pallas_call(
        _inproj_kernel,
        grid=(T // tm, N // tn),
        in_specs=[
            pl.BlockSpec((tm, D), lambda i, j: (i, 0)),
            pl.BlockSpec((1, D), lambda i, j: (0, 0)),
            pl.BlockSpec((D, tn), lambda i, j: (0, j)),
        ],
        out_specs=pl.BlockSpec((tm, tn), lambda i, j: (i, j)),
        out_shape=jax.ShapeDtypeStruct((T, N), F32),
        scratch_shapes=[pltpu.VMEM((tm, D), BF16)],
        compiler_params=_cparams(("parallel", "arbitrary")),
    )(x, norm_w.reshape(1, D), w)


def _sb_kernel(q_ref, k_ref, v_ref, mcat_ref, o_ref, *, blk):
    qi = pl.program_id(2)
    lane = lax.broadcasted_iota(jnp.int32, (1, LANES), 1)
    head_mask = [lane < SB_HEAD_DIM, lane >= SB_HEAD_DIM]
    q = q_ref[...] * (SB_HEAD_DIM ** -0.5)
    qh = [jnp.where(head_mask[h], q, 0.0).astype(BF16) for h in range(2)]
    row = lax.broadcasted_iota(jnp.int32, (blk, blk), 0)
    col = lax.broadcasted_iota(jnp.int32, (blk, blk), 1)
    before = col < row
    mcat = mcat_ref[...]

    def block(kb, carry, diag):
        c0, c1, acc = carry
        cs_in = [c0, c1]
        cs_out = []
        ks = pl.multiple_of(kb * blk, blk)
        kblk = k_ref[pl.ds(ks, blk), :].astype(BF16)
        vblk = v_ref[pl.ds(ks, blk), :]
        for h in range(2):
            z = _dot_nt(qh[h], kblk)
            t = jnp.log1p(jnp.exp(-jnp.abs(z)))
            log_beta = jnp.minimum(z, 0.0) - t
            log_rest = jnp.minimum(-z, 0.0) - t
            if diag:
                log_rest = jnp.where(before, log_rest, 0.0)
            hi = log_rest.astype(BF16)
            lo = (log_rest - hi.astype(F32)).astype(BF16)
            cs = _dot(jnp.concatenate([hi, lo], axis=1), mcat)
            a = jnp.exp(log_beta + cs[:, :blk] + cs_in[h])
            if diag:
                a = jnp.where(before, a, 0.0)
            vh = jnp.where(head_mask[h], vblk, 0.0).astype(BF16)
            acc = acc + _dot(a.astype(BF16), vh)
            cs_out.append(cs_in[h] + cs[:, blk:])
        return cs_out[0], cs_out[1], acc

    zero = jnp.zeros((blk, blk), F32)
    carry = block(qi, (zero, zero, jnp.zeros((blk, LANES), F32)), True)
    carry = lax.fori_loop(0, qi, lambda i, c: block(qi - 1 - i, c, False), carry)
    o_ref[...] = carry[2]


def _sb_consts(blk):
    r = np.arange(2 * blk)[:, None] % blk
    c = np.arange(2 * blk)[None, :]
    return jnp.asarray(((c >= blk) | (r > c)).astype(np.float32), dtype=BF16)


def _sb_attention(proj, batch, seq, blk=128):
    T = proj.shape[0]
    nq = seq // blk
    cb = COL_SB // LANES
    npair = SB_WIDTH // LANES
    return pl.pallas_call(
        functools.partial(_sb_kernel, blk=blk),
        grid=(batch, npair, nq),
        in_specs=[
            pl.BlockSpec((blk, LANES), lambda b, p, i: (b * nq + i, cb + p)),
            pl.BlockSpec((seq, LANES), lambda b, p, i: (b, cb + npair + p)),
            pl.BlockSpec((seq, LANES), lambda b, p, i: (b, cb + 2 * npair + p)),
            pl.BlockSpec((2 * blk, 2 * blk), lambda b, p, i: (0, 0)),
        ],
        out_specs=pl.BlockSpec((blk, LANES), lambda b, p, i: (b * nq + i, p)),
        out_shape=jax.ShapeDtypeStruct((T, SB_WIDTH), F32),
        compiler_params=_cparams(("parallel", "parallel", "arbitrary")),
    )(proj, proj, proj, _sb_consts(blk))


HG_LEVELS = (32, 16, 8, 4, 2, 1)


def _hg_consts():
    C = HG_CHUNK
    i = np.arange(C)
    sels = [i[None, :] <= i[:, None]]
    masks = []
    for m in HG_LEVELS:
        g = (i // (2 * m)) * (2 * m) + m - 1
        sels.append(i[None, :] <= g[:, None])
        same = (i[:, None] // (2 * m)) == (i[None, :] // (2 * m))
        upper_t = (i[:, None] % (2 * m)) >= m
        lower_s = (i[None, :] % (2 * m)) < m
        masks.append(same & upper_t & lower_s)
    masks.append(np.eye(C, dtype=bool))
    big = np.concatenate(sels, 0).astype(np.float32)
    big3 = np.concatenate([big, big, big], 1)
    return jnp.asarray(big3, dtype=BF16), jnp.asarray(np.stack(masks).astype(np.float32))


def _hg_kernel(f_ref, q_ref, i_ref, g_ref, lb_ref, nw_ref, sel_ref, msk_ref, o_ref, st_ref, *, nchunk):
    C = HG_CHUNK

    @pl.when(pl.program_id(1) == 0)
    def _():
        st_ref[...] = jnp.zeros_like(st_ref)

    sel = sel_ref[...]
    nw = nw_ref[...]
    for c in range(nchunk):
        rows = slice(c * C, (c + 1) * C)
        for h in range(HG_HEADS):
            cols = slice(h * HG_DK, (h + 1) * HG_DK)
            lb = lb_ref[:, cols]
            sp, sn = _sigmoid_pair(f_ref[rows, cols])
            f = lb + (1.0 - lb) * sp
            log_f = jnp.log(jnp.maximum(f, HG_F_FLOOR))
            key = (1.0 - lb) * sn
            q = _silu(q_ref[rows, cols]) * (HG_DK ** -0.5)
            val = i_ref[rows, cols].astype(BF16)
            br = _dot(sel, _split3(log_f, 0))
            b = br[:C]
            s = msk_ref[len(HG_LEVELS)] * _dot_nt(q.astype(BF16), key.astype(BF16))
            for lvl in range(len(HG_LEVELS)):
                r = br[(lvl + 1) * C:(lvl + 2) * C]
                qm = (q * jnp.exp(jnp.minimum(b - r, 0.0))).astype(BF16)
                km = (key * jnp.exp(jnp.minimum(r - b, 0.0))).astype(BF16)
                s = s + msk_ref[lvl] * _dot_nt(qm, km)
            st = st_ref[h]
            o = _dot(s.astype(BF16), val)
            o = o + _dot_nt((q * jnp.exp(b)).astype(BF16), st.astype(BF16))
            b_last = b[C - 1:C, :]
            kt = (key * jnp.exp(b_last - b)).astype(BF16)
            st_ref[h] = st * jnp.exp(b_last) + _dot_tn(val, kt)
            ms = jnp.mean(o * o, axis=-1, keepdims=True)
            o = o * lax.rsqrt(ms + EPS) * nw
            o_ref[rows, cols] = o * _silu(g_ref[rows, cols])


def _hgrn2(proj, lb, norm_w, batch, seq, ct):
    T = proj.shape[0]
    W = HG_HEADS * HG_DK
    nc = seq // ct
    sel, msk = _hg_consts()
    cb = COL_HG // W
    spec = lambda k: pl.BlockSpec((ct, W), lambda b, c: (b * nc + c, cb + k))
    return pl.pallas_call(
        functools.partial(_hg_kernel, nchunk=ct // HG_CHUNK),
        grid=(batch, nc),
        in_specs=[spec(0), spec(1), spec(2), spec(3),
                  pl.BlockSpec((1, W), lambda b, c: (0, 0)),
                  pl.BlockSpec((1, HG_DK), lambda b, c: (0, 0)),
                  pl.BlockSpec(sel.shape, lambda b, c: (0, 0)),
                  pl.BlockSpec(msk.shape, lambda b, c: (0, 0, 0))],
        out_specs=pl.BlockSpec((ct, W), lambda b, c: (b * nc + c, 0)),
        out_shape=jax.ShapeDtypeStruct((T, W), F32),
        scratch_shapes=[pltpu.VMEM((HG_HEADS, HG_DK, HG_DK), F32)],
        compiler_params=_cparams(("parallel", "arbitrary")),
    )(proj, proj, proj, proj, lb.reshape(1, W), norm_w.reshape(1, HG_DK), sel, msk)


SSD_PAD = 8


def _ssd_kernel(z_ref, xbc_ref, dt_ref, cw_ref, cb_ref, dtb_ref, a_ref, d_ref, nw_ref, tril3_ref,
                o_ref, pad_ref, st_ref):
    L = SSD_CHUNK
    W = SSD_WIDTH
    GN = 2 * SSD_STATE

    @pl.when(pl.program_id(1) == 0)
    def _():
        pad_ref[0:SSD_PAD, :] = jnp.zeros((SSD_PAD, SSD_CONV_DIM), F32)
        st_ref[...] = jnp.zeros_like(st_ref)

    pad_ref[SSD_PAD:SSD_PAD + L, :] = xbc_ref[...]
    conv = cb_ref[...]
    for k in range(SSD_CONV):
        off = SSD_PAD - (SSD_CONV - 1) + k
        conv = conv + cw_ref[k:k + 1, :] * pad_ref[off:off + L, :]
    pad_ref[0:SSD_PAD, :] = pad_ref[L:L + SSD_PAD, :]
    xa = _silu(conv)
    xs = xa[:, :W]
    bm = xa[:, W:W + GN]
    cm = xa[:, W + GN:W + 2 * GN]
    dtv = dt_ref[...] + dtb_ref[...]
    dt = jnp.maximum(dtv, 0.0) + jnp.log1p(jnp.exp(-jnp.abs(dtv)))
    a = a_ref[...] * dt
    tril3 = tril3_ref[...]
    a_cum = _dot(tril3, _split3(a, 0))
    a_last = a_cum[L - 1:L, :]
    xdt = xs * dt

    lane = lax.broadcasted_iota(jnp.int32, (1, LANES), 1)
    half = [lane < SSD_STATE, lane >= SSD_STATE]
    row = lax.broadcasted_iota(jnp.int32, (L, L), 0)
    col = lax.broadcasted_iota(jnp.int32, (L, L), 1)
    strict = col < row
    causal = col <= row
    bb = bm.astype(BF16)
    cbm = cm.astype(BF16)
    cb_g = [_dot_nt(jnp.where(half[g], cm, 0.0).astype(BF16), bb) for g in range(2)]

    y_parts = []
    for pair in range(SSD_HEADS // 2):
        g = pair // 2
        xp = xdt[:, pair * LANES:(pair + 1) * LANES]
        acc = jnp.zeros((L, LANES), F32)
        for hh in range(2):
            h = 2 * pair + hh
            a_col = a[:, h * SSD_HEAD_DIM:h * SSD_HEAD_DIM + 1]
            diff = _dot(tril3, _split3(jnp.where(strict, a_col, 0.0), 0))
            seg = jnp.exp(jnp.where(causal, diff, NEG_BIG))
            sc = (cb_g[g] * seg).astype(BF16)
            acc = acc + _dot(sc, jnp.where(half[hh], xp, 0.0).astype(BF16))
        y_parts.append(acc)
    y = jnp.concatenate(y_parts, axis=1)

    st = st_ref[...]
    y = y + jnp.exp(a_cum) * _dot(cbm, st.astype(BF16))
    srow = lax.broadcasted_iota(jnp.int32, (GN, W), 0) // SSD_STATE
    scol = lax.broadcasted_iota(jnp.int32, (GN, W), 1) // (W // 2)
    upd = _dot_tn(bb, (jnp.exp(a_last - a_cum) * xdt).astype(BF16))
    st_ref[...] = st * jnp.exp(a_last) + jnp.where(srow == scol, upd, 0.0)

    y = y + d_ref[...] * xs
    y = y * _silu(z_ref[...])
    nw = nw_ref[...]
    outs = []
    for g in range(2):
        yg = y[:, g * (W // 2):(g + 1) * (W // 2)]
        ms = jnp.mean(yg * yg, axis=-1, keepdims=True)
        outs.append(yg * lax.rsqrt(ms + EPS) * nw[:, g * (W // 2):(g + 1) * (W // 2)])
    o_ref[...] = jnp.concatenate(outs, axis=1)


def _ssd(proj, conv_w, conv_b, dt_bias, a_log, d_skip, norm_w, batch, seq):
    T = proj.shape[0]
    L = SSD_CHUNK
    W = SSD_WIDTH
    nc = seq // L
    rep = lambda v: jnp.repeat(v.astype(F32), SSD_HEAD_DIM).reshape(1, W)
    tril = np.tril(np.ones((L, L), np.float32))
    tril3 = jnp.asarray(np.concatenate([tril, tril, tril], 1), dtype=BF16)
    vec = lambda n: pl.BlockSpec((1, n), lambda b, c: (0, 0))
    return pl.pallas_call(
        _ssd_kernel,
        grid=(batch, nc),
        in_specs=[
            pl.BlockSpec((L, W), lambda b, c: (b * nc + c, COL_Z // W)),
            pl.BlockSpec((L, SSD_CONV_DIM), lambda b, c: (b * nc + c, COL_XBC // SSD_CONV_DIM)),
            pl.BlockSpec((L, W), lambda b, c: (b * nc + c, COL_DT // W)),
            pl.BlockSpec((SSD_CONV, SSD_CONV_DIM), lambda b, c: (0, 0)),
            vec(SSD_CONV_DIM), vec(W), vec(W), vec(W), vec(W),
            pl.BlockSpec((L, 3 * L), lambda b, c: (0, 0)),
        ],
        out_specs=pl.BlockSpec((L, W), lambda b, c: (b * nc + c, 0)),
        out_shape=jax.ShapeDtypeStruct((T, W), F32),
        scratch_shapes=[pltpu.VMEM((L + SSD_PAD, SSD_CONV_DIM), F32),
                        pltpu.VMEM((2 * SSD_STATE, W), F32)],
        compiler_params=_cparams(("parallel", "arbitrary")),
    )(proj, proj, proj, conv_w, conv_b.reshape(1, SSD_CONV_DIM), rep(dt_bias),
      rep(-jnp.exp(a_log.astype(F32))), rep(d_skip), norm_w.reshape(1, W), tril3)


def _merge_kernel(ysb_ref, yhg_ref, yssd_ref, g0_ref, g1_ref, g2_ref, x_ref,
                  w0_ref, w1_ref, w2_ref, wo_ref, nw_ref, xo_ref, h_ref):
    m = _sigmoid_pair(g0_ref[...])[0] * _dot(ysb_ref[...].astype(BF16), w0_ref[...])
    m = m + _sigmoid_pair(g1_ref[...])[0] * _dot(yhg_ref[...].astype(BF16), w1_ref[...])
    m = m + _sigmoid_pair(g2_ref[...])[0] * _dot(yssd_ref[...].astype(BF16), w2_ref[...])
    xn = x_ref[...] + _dot(m.astype(BF16), wo_ref[...])
    xo_ref[...] = xn
    ms = jnp.mean(xn * xn, axis=-1, keepdims=True)
    h_ref[...] = (xn * lax.rsqrt(ms + EPS) * nw_ref[...]).astype(BF16)


def _merge(y_sb, y_hg, y_ssd, proj, x, w_sb, w_hg, w_ssd, w_out, norm_w, tm):
    T, D = x.shape
    Wb = y_sb.shape[1]
    yspec = pl.BlockSpec((tm, Wb), lambda i: (i, 0))
    gspec = lambda k: pl.BlockSpec((tm, D), lambda i: (i, COL_GATE // D + k))
    wspec = pl.BlockSpec((Wb, D), lambda i: (0, 0))
    xspec = pl.BlockSpec((tm, D), lambda i: (i, 0))
    return pl.pallas_call(
        _merge_kernel,
        grid=(T // tm,),
        in_specs=[yspec, yspec, yspec, gspec(0), gspec(1), gspec(2), xspec,
                  wspec, wspec, wspec,
                  pl.BlockSpec((D, D), lambda i: (0, 0)),
                  pl.BlockSpec((1, D), lambda i: (0, 0))],
        out_specs=[xspec, xspec],
        out_shape=[jax.ShapeDtypeStruct((T, D), F32), jax.ShapeDtypeStruct((T, D), BF16)],
        compiler_params=_cparams(("parallel",)),
    )(y_sb, y_hg, y_ssd, proj, proj, proj, x, w_sb, w_hg, w_ssd, w_out, norm_w.reshape(1, D))


def _peerq_kernel(h_ref, wq_ref, sk_ref, s1_ref, s2_ref, ht_ref):
    h = h_ref[...]
    q = _dot(h, wq_ref[...]).astype(BF16)
    for hd in range(PEER_HEADS):
        for p, s_ref in enumerate((s1_ref, s2_ref)):
            j = hd * 2 + p
            s_ref[hd] = _dot_nt(sk_ref[j], q[:, j * PEER_NKEYS:(j + 1) * PEER_NKEYS])
    ht_ref[...] = h.astype(F32).T.astype(BF16)


def _peer_query(h2, w_q, sub_keys, tm):
    T, D = h2.shape
    NK = PEER_NKEYS
    sspec = pl.BlockSpec((PEER_HEADS, NK, tm), lambda i: (0, 0, i))
    return pl.pallas_call(
        _peerq_kernel,
        grid=(T // tm,),
        in_specs=[pl.BlockSpec((tm, D), lambda i: (i, 0)),
                  pl.BlockSpec(w_q.shape, lambda i: (0, 0)),
                  pl.BlockSpec(sub_keys.shape, lambda i: (0, 0, 0))],
        out_specs=[sspec, sspec, pl.BlockSpec((D, tm), lambda i: (0, i))],
        out_shape=[jax.ShapeDtypeStruct((PEER_HEADS, NK, T), F32),
                   jax.ShapeDtypeStruct((PEER_HEADS, NK, T), F32),
                   jax.ShapeDtypeStruct((D, T), BF16)],
        compiler_params=_cparams(("parallel",)),
    )(h2, w_q, sub_keys)


def _topvals(s_ref, v_ref, c_ref):
    x = s_ref[0]
    for r in range(PEER_TOPK):
        m = jnp.max(x, axis=0, keepdims=True)
        eq = x == m
        v_ref[r:r + 1, :] = m
        c_ref[r:r + 1, :] = jnp.sum(jnp.where(eq, 1.0, 0.0), axis=0, keepdims=True)
        x = jnp.where(eq, -jnp.inf, x)


def _peer_topk_kernel(s1_ref, s2_ref, o_ref, v1_ref, c1_ref, v2_ref, c2_ref):
    K = PEER_TOPK
    _topvals(s1_ref, v1_ref, c1_ref)
    _topvals(s2_ref, v2_ref, c2_ref)
    v1 = v1_ref[...]
    c1 = c1_ref[...]
    cands, wts = [], []
    for j in range(K):
        nj = K if j == 0 else 8
        cands.append(v1[:nj] + v2_ref[j:j + 1, :])
        wts.append(c1[:nj] * c2_ref[j:j + 1, :])
    cand = jnp.concatenate(cands, axis=0)
    wt = jnp.concatenate(wts, axis=0)
    m0 = v1[0:1] + v2_ref[0:1, :]
    total = jnp.zeros_like(m0)
    tau = m0
    zsum = jnp.zeros_like(m0)
    for r in range(K):
        m = jnp.max(cand, axis=0, keepdims=True)
        eq = cand == m
        cnt = jnp.sum(jnp.where(eq, wt, 0.0), axis=0, keepdims=True)
        need = total < K
        take = jnp.minimum(cnt, K - total)
        zsum = zsum + jnp.where(need, take * jnp.exp(m - m0), 0.0)
        tau = jnp.where(need, m, tau)
        total = total + cnt
        cand = jnp.where(eq, -jnp.inf, cand)
    o_ref[0, 0:1, :] = tau
    o_ref[0, 1:2, :] = v1[0:1]
    o_ref[0, 2:3, :] = v2_ref[0:1, :]
    o_ref[0, 3:4, :] = 1.0 / zsum


def _peer_topk(s1t, s2t, tt):
    H, NK, T = s1t.shape
    sspec = pl.BlockSpec((1, NK, tt), lambda i, h: (h, 0, i))
    scr = pltpu.VMEM((PEER_TOPK, tt), F32)
    return pl.pallas_call(
        _peer_topk_kernel,
        grid=(T // tt, H),
        in_specs=[sspec, sspec],
        out_specs=pl.BlockSpec((1, 4, tt), lambda i, h: (h, 0, i)),
        out_shape=jax.ShapeDtypeStruct((H, 4, T), F32),
        scratch_shapes=[scr, scr, scr, scr],
        compiler_params=_cparams(("parallel", "parallel")),
    )(s1t, s2t)


def _peer_kernel(x_ref, ht_ref, s1_ref, s2_ref, st_ref, u_ref, vt_ref, fw_ref, o_ref,
                 r1_ref, e2_ref, act_ref, p_ref, acc_ref, *, eb, tt, final):
    e = pl.program_id(1)
    NK = PEER_NKEYS

    @pl.when(e == 0)
    def _():
        for h in range(PEER_HEADS):
            r1_ref[h] = jnp.exp(s1_ref[h] - st_ref[h, 1:2, :]) * st_ref[h, 3:4, :]
            e2_ref[h] = jnp.exp(s2_ref[h] - st_ref[h, 2:3, :])
        acc_ref[...] = jnp.zeros_like(acc_ref)

    act_ref[...] = _dot(u_ref[...], ht_ref[...])

    assert eb // NK == SUBLANES
    a0 = pl.multiple_of(e * SUBLANES, SUBLANES)

    def t_body(tc, _):
        ts = pl.ds(pl.multiple_of(tc * LANES, LANES), LANES)
        for ai in range(SUBLANES):
            w = jnp.zeros((NK, LANES), F32)
            for h in range(PEER_HEADS):
                s1g = s1_ref[h, pl.ds(a0, SUBLANES), ts]
                r1g = r1_ref[h, pl.ds(a0, SUBLANES), ts]
                tot = s2_ref[h, :, ts] + s1g[ai:ai + 1, :]
                gate = e2_ref[h, :, ts] * r1g[ai:ai + 1, :]
                w = w + jnp.where(tot >= st_ref[h, 0:1, ts], gate, 0.0)
            act = act_ref[ai * NK:(ai + 1) * NK, ts]
            gelu = 0.5 * act * (1.0 + lax.erf(act * (2.0 ** -0.5)))
            p_ref[ai * NK:(ai + 1) * NK, ts] = (w * gelu).astype(BF16)
        return 0

    lax.fori_loop(0, tt // LANES, t_body, 0)
    acc_ref[...] += _dot(vt_ref[...], p_ref[...])

    @pl.when(e == pl.num_programs(1) - 1)
    def _():
        y = x_ref[...] + acc_ref[...].T
        if final:
            ms = jnp.mean(y * y, axis=-1, keepdims=True)
            y = y * lax.rsqrt(ms + EPS) * fw_ref[...]
        o_ref[...] = y


def _peer(x, h2t, s1t, s2t, stats, u, vt, final_w, tt, eb, final):
    T, D = x.shape
    E = u.shape[0]
    H, NK = PEER_HEADS, PEER_NKEYS
    sspec = pl.BlockSpec((H, NK, tt), lambda i, e: (0, 0, i))
    return pl.pallas_call(
        functools.partial(_peer_kernel, eb=eb, tt=tt, final=final),
        grid=(T // tt, E // eb),
        in_specs=[pl.BlockSpec((tt, D), lambda i, e: (i, 0)),
                  pl.BlockSpec((D, tt), lambda i, e: (0, i)),
                  sspec, sspec,
                  pl.BlockSpec((H, 4, tt), lambda i, e: (0, 0, i)),
                  pl.BlockSpec((eb, D), lambda i, e: (e, 0)),
                  pl.BlockSpec((D, eb), lambda i, e: (0, e)),
                  pl.BlockSpec((1, D), lambda i, e: (0, 0))],
        out_specs=pl.BlockSpec((tt, D), lambda i, e: (i, 0)),
        out_shape=jax.ShapeDtypeStruct((T, D), F32),
        scratch_shapes=[pltpu.VMEM((H, NK, tt), F32), pltpu.VMEM((H, NK, tt), F32),
                        pltpu.VMEM((eb, tt), F32), pltpu.VMEM((eb, tt), BF16),
                        pltpu.VMEM((D, tt), F32)],
        compiler_params=_cparams(("parallel", "arbitrary")),
    )(x, h2t, s1t, s2t, stats, u, vt, final_w.reshape(1, D))


def _layer(x, batch, seq, lb, p, final_w, final, tiles):
    proj = _inproj(x, p["norm1_w"], p["w_in"], tiles["in_tm"], tiles["in_tn"])
    y_sb = _sb_attention(proj, batch, seq, tiles["sb_blk"])
    y_hg = _hgrn2(proj, lb, p["hg_norm_w"], batch, seq, tiles["hg_ct"])
    y_ssd = _ssd(proj, p["ssd_conv_w"], p["ssd_conv_b"], p["ssd_dt_bias"], p["ssd_a_log"],
                 p["ssd_d"], p["ssd_norm_w"], batch, seq)
    x, h2 = _merge(y_sb, y_hg, y_ssd, proj, x, p["w_branch_sb"], p["w_branch_hg"],
                   p["w_branch_ssd"], p["w_out"], p["norm2_w"], tiles["mg_tm"])
    s1t, s2t, h2t = _peer_query(h2, p["peer_w_q"], p["peer_sub_keys"], tiles["pq_tm"])
    stats = _peer_topk(s1t, s2t, tiles["tk_tt"])
    return _peer(x, h2t, s1t, s2t, stats, p["peer_u"], p["peer_vt"], final_w,
                 tiles["pe_tt"], tiles["pe_eb"], final)


TILES = dict(in_tm=1024, in_tn=768, sb_blk=128, hg_ct=256, mg_tm=256, pq_tm=512,
             tk_tt=256, pe_tt=512, pe_eb=1024)


def _prep_w_in(w):
    c = np.cumsum([0, 512, 512, 512, 512, 512, 512, 512, 512, 768, 8, 1024, 1024, 1024])
    dt_rep = jnp.repeat(w[:, c[9]:c[10]], SSD_HEAD_DIM, axis=1)
    return jnp.concatenate([w[:, c[3]:c[7]], w[:, c[7]:c[8]], dt_rep, w[:, c[10]:c[13]],
                            w[:, c[8]:c[9]], w[:, c[0]:c[3]]], axis=1).astype(BF16)


def _trunk(x, norm1_w, w_in, hg_lb_logits, hg_norm_w, ssd_conv_w, ssd_conv_b, ssd_dt_bias,
           ssd_a_log, ssd_d, ssd_norm_w, w_branch_sb, w_branch_hg, w_branch_ssd, w_out,
           norm2_w, peer_w_q, peer_sub_keys, peer_u, peer_v, final_norm_w, tiles):
    batch, seq, d = x.shape
    depth = w_in.shape[0]
    gamma = jax.nn.softmax(hg_lb_logits.astype(F32), axis=0)
    lower_bounds = jnp.cumsum(gamma, axis=0) - gamma[0]
    xt = x.reshape(batch * seq, d)
    for l in range(depth):
        p = dict(
            norm1_w=norm1_w[l], w_in=_prep_w_in(w_in[l]), hg_norm_w=hg_norm_w[l],
            ssd_conv_w=ssd_conv_w[l], ssd_conv_b=ssd_conv_b[l], ssd_dt_bias=ssd_dt_bias[l],
            ssd_a_log=ssd_a_log[l], ssd_d=ssd_d[l], ssd_norm_w=ssd_norm_w[l],
            w_branch_sb=w_branch_sb[l].astype(BF16), w_branch_hg=w_branch_hg[l].astype(BF16),
            w_branch_ssd=w_branch_ssd[l].astype(BF16), w_out=w_out[l].astype(BF16),
            norm2_w=norm2_w[l], peer_w_q=peer_w_q[l].astype(BF16),
            peer_sub_keys=peer_sub_keys[l].reshape(2 * PEER_HEADS, PEER_NKEYS, -1).astype(BF16),
            peer_u=peer_u[l].astype(BF16), peer_vt=peer_v[l].T.astype(BF16))
        xt = _layer(xt, batch, seq, lower_bounds[l], p, final_norm_w, l == depth - 1, tiles)
    return xt.reshape(batch, seq, d)


def kernel(x, norm1_w, w_in, hg_lb_logits, hg_norm_w, ssd_conv_w, ssd_conv_b, ssd_dt_bias, ssd_a_log, ssd_d, ssd_norm_w, w_branch_sb, w_branch_hg, w_branch_ssd, w_out, norm2_w, peer_w_q, peer_sub_keys, peer_u, peer_v, final_norm_w):
    return _trunk(x, norm1_w, w_in, hg_lb_logits, hg_norm_w, ssd_conv_w, ssd_conv_b, ssd_dt_bias,
                  ssd_a_log, ssd_d, ssd_norm_w, w_branch_sb, w_branch_hg, w_branch_ssd, w_out,
                  norm2_w, peer_w_q, peer_sub_keys, peer_u, peer_v, final_norm_w, TILES)
```

```python
import functools

import numpy as np
import jax
import jax.numpy as jnp
from jax import lax
from jax.experimental import pallas as pl
from jax.experimental.pallas import tpu as pltpu

F32 = jnp.float32
BF16 = jnp.bfloat16

D_MODEL = 1024
SB_HEAD_DIM = 64
SB_WIDTH = 512
HG_HEADS = 4
HG_DK = 128
HG_CHUNK = 64
HG_F_FLOOR = 1e-30
SSD_HEADS = 8
SSD_HEAD_DIM = 64
SSD_WIDTH = 512
SSD_STATE = 64
SSD_CONV = 4
SSD_CHUNK = 128
SSD_CONV_DIM = 768
PEER_HEADS = 8
PEER_NKEYS = 128
PEER_TOPK = 16
EPS = 1e-6
NEG_BIG = -1e30

LANES = 128
SUBLANES = 8
VMEM_LIMIT = 56 * 1024 * 1024

COL_HG = 0
COL_Z = 2048
COL_DT = 2560
COL_GATE = 3072
COL_XBC = 6144
COL_SB = 6912
IN_COLS_PADDED = 8448


def _cparams(sem):
    return pltpu.CompilerParams(dimension_semantics=sem, vmem_limit_bytes=VMEM_LIMIT)


def _dot(a, b):
    return jnp.dot(a, b, preferred_element_type=F32)


def _dot_nt(a, b):
    return lax.dot_general(a, b, (((1,), (1,)), ((), ())), preferred_element_type=F32)


def _dot_tn(a, b):
    return lax.dot_general(a, b, (((0,), (0,)), ((), ())), preferred_element_type=F32)


def _sigmoid_pair(x):
    e = jnp.exp(-jnp.abs(x))
    inv = 1.0 / (1.0 + e)
    small = e * inv
    pos = x >= 0
    return jnp.where(pos, inv, small), jnp.where(pos, small, inv)


def _silu(x):
    return x * _sigmoid_pair(x)[0]


def _split3(x, axis):
    hi = x.astype(BF16)
    r1 = x - hi.astype(F32)
    mid = r1.astype(BF16)
    lo = (r1 - mid.astype(F32)).astype(BF16)
    return jnp.concatenate([hi, mid, lo], axis=axis)


def _inproj_kernel(x_ref, nw_ref, w_ref, o_ref, h_ref):
    @pl.when(pl.program_id(1) == 0)
    def _():
        x = x_ref[...]
        ms = jnp.mean(x * x, axis=-1, keepdims=True)
        h_ref[...] = (x * lax.rsqrt(ms + EPS) * nw_ref[...]).astype(BF16)

    o_ref[...] = _dot(h_ref[...], w_ref[...])


def _inproj(x, norm_w, w, tm, tn):
    T, D = x.shape
    N = w.shape[1]
    return pl.pallas_call(
        _inproj_kernel,
        grid=(T // tm, N // tn),
        in_specs=[
            pl.BlockSpec((tm, D), lambda i, j: (i, 0)),
            pl.BlockSpec((1, D), lambda i, j: (0, 0)),
            pl.BlockSpec((D, tn), lambda i, j: (0, j)),
        ],
        out_specs=pl.BlockSpec((tm, tn), lambda i, j: (i, j)),
        out_shape=jax.ShapeDtypeStruct((T, N), F32),
        scratch_shapes=[pltpu.VMEM((tm, D), BF16)],
        compiler_params=_cparams(("parallel", "arbitrary")),
    )(x, norm_w.reshape(1, D), w)


SB_EXIT = -110.0


def _sb_kernel(q_ref, k_ref, v_ref, mcat_ref, o_ref, c_ref, acc_ref, *, blk, nsub):
    qi = pl.program_id(2)
    lane = lax.broadcasted_iota(jnp.int32, (1, LANES), 1)
    first_head = lane < SB_HEAD_DIM
    mcat = mcat_ref[...]
    qparts = []
    for s in range(nsub):
        q = q_ref[s * blk:(s + 1) * blk, :] * (SB_HEAD_DIM ** -0.5)
        qparts += [jnp.where(first_head, q, 0.0), jnp.where(first_head, 0.0, q)]
    qs = jnp.concatenate(qparts, axis=0).astype(BF16)
    c_ref[...] = jnp.zeros_like(c_ref)
    acc_ref[...] = jnp.zeros_like(acc_ref)

    def visit(kb, s0, diag):
        r0 = s0 * 2 * blk
        ks = pl.multiple_of(kb * blk, blk)
        kblk = k_ref[pl.ds(ks, blk), :].astype(BF16)
        vblk = v_ref[pl.ds(ks, blk), :].astype(BF16)
        z = _dot_nt(qs[r0:], kblk)
        t = jnp.log1p(jnp.exp(-jnp.abs(z)))
        log_beta = jnp.minimum(z, 0.0) - t
        log_rest = jnp.minimum(-z, 0.0) - t
        if diag:
            row = lax.broadcasted_iota(jnp.int32, z.shape, 0)
            col = lax.broadcasted_iota(jnp.int32, z.shape, 1)
            before = (row >= 2 * blk) | (col < (row & (blk - 1)))
            log_rest = jnp.where(before, log_rest, 0.0)
        hi = log_rest.astype(BF16)
        lo = (log_rest - hi.astype(F32)).astype(BF16)
        cs = _dot(jnp.concatenate([hi, lo], axis=1), mcat)
        carry = c_ref[r0:, :]
        a = jnp.exp(log_beta + cs[:, :blk] + carry)
        if diag:
            a = jnp.where(before, a, 0.0)
        carry = carry + cs[:, blk:]
        c_ref[r0:, :] = carry
        av = _dot(a.astype(BF16), vblk)
        for s in range(s0, nsub):
            i = (s - s0) * 2 * blk
            acc_ref[s * blk:(s + 1) * blk, :] += jnp.where(first_head, av[i:i + blk], av[i + blk:i + 2 * blk])
        return carry

    for j in range(nsub - 1, -1, -1):
        carry = visit(qi * nsub + j, j, True)

    def body(state):
        kb, _ = state
        return kb - 1, jnp.max(visit(kb, 0, False))

    lax.while_loop(lambda st: (st[0] >= 0) & (st[1] > SB_EXIT), body,
                   (qi * nsub - 1, jnp.max(carry)))
    o_ref[...] = acc_ref[...]


def _sb_consts(blk):
    r = np.arange(2 * blk)[:, None] % blk
    c = np.arange(2 * blk)[None, :]
    return jnp.asarray(((c >= blk) | (r > c)).astype(np.float32), dtype=BF16)


def _sb_attention(proj, batch, seq, blk, nsub):
    T = proj.shape[0]
    tq = blk * nsub
    nq = seq // tq
    cb = COL_SB // LANES
    npair = SB_WIDTH // LANES
    return pl.pallas_call(
        functools.partial(_sb_kernel, blk=blk, nsub=nsub),
        grid=(batch, npair, nq),
        in_specs=[
            pl.BlockSpec((tq, LANES), lambda b, p, i: (b * nq + i, cb + p)),
            pl.BlockSpec((seq, LANES), lambda b, p, i: (b, cb + npair + p)),
            pl.BlockSpec((seq, LANES), lambda b, p, i: (b, cb + 2 * npair + p)),
            pl.BlockSpec((2 * blk, 2 * blk), lambda b, p, i: (0, 0)),
        ],
        out_specs=pl.BlockSpec((tq, LANES), lambda b, p, i: (b * nq + i, p)),
        out_shape=jax.ShapeDtypeStruct((T, SB_WIDTH), F32),
        scratch_shapes=[pltpu.VMEM((nsub * 2 * blk, blk), F32), pltpu.VMEM((tq, LANES), F32)],
        compiler_params=_cparams(("parallel", "parallel", "arbitrary")),
    )(proj, proj, proj, _sb_consts(blk))


HG_LEVELS = (32, 16, 8, 4, 2, 1)


def _hg_consts():
    C = HG_CHUNK
    i = np.arange(C)
    sels = [i[None, :] <= i[:, None]]
    masks = []
    for m in HG_LEVELS:
        g = (i // (2 * m)) * (2 * m) + m - 1
        sels.append(i[None, :] <= g[:, None])
        same = (i[:, None] // (2 * m)) == (i[None, :] // (2 * m))
        upper_t = (i[:, None] % (2 * m)) >= m
        lower_s = (i[None, :] % (2 * m)) < m
        masks.append(same & upper_t & lower_s)
    masks.append(np.eye(C, dtype=bool))
    big = np.concatenate(sels, 0).astype(np.float32)
    big3 = np.concatenate([big, big, big], 1)
    return jnp.asarray(big3, dtype=BF16), jnp.asarray(np.stack(masks).astype(np.float32))


def _hg_kernel(f_ref, q_ref, i_ref, g_ref, lb_ref, nw_ref, sel_ref, msk_ref, o_ref, st_ref, *, nchunk):
    C = HG_CHUNK

    @pl.when(pl.program_id(1) == 0)
    def _():
        st_ref[...] = jnp.zeros_like(st_ref)

    sel = sel_ref[...]
    nw = nw_ref[...]
    for c in range(nchunk):
        rows = slice(c * C, (c + 1) * C)
        for h in range(HG_HEADS):
            cols = slice(h * HG_DK, (h + 1) * HG_DK)
            lb = lb_ref[:, cols]
            sp, sn = _sigmoid_pair(f_ref[rows, cols])
            f = lb + (1.0 - lb) * sp
            log_f = jnp.log(jnp.maximum(f, HG_F_FLOOR))
            key = (1.0 - lb) * sn
            q = _silu(q_ref[rows, cols]) * (HG_DK ** -0.5)
            val = i_ref[rows, cols].astype(BF16)
            br = _dot(sel, _split3(log_f, 0))
            b = br[:C]
            s = msk_ref[len(HG_LEVELS)] * _dot_nt(q.astype(BF16), key.astype(BF16))
            for lvl in range(len(HG_LEVELS)):
                r = br[(lvl + 1) * C:(lvl + 2) * C]
                qm = (q * jnp.exp(jnp.minimum(b - r, 0.0))).astype(BF16)
                km = (key * jnp.exp(jnp.minimum(r - b, 0.0))).astype(BF16)
                s = s + msk_ref[lvl] * _dot_nt(qm, km)
            st = st_ref[h]
            o = _dot(s.astype(BF16), val)
            o = o + _dot_nt((q * jnp.exp(b)).astype(BF16), st.astype(BF16))
            b_last = b[C - 1:C, :]
            kt = (key * jnp.exp(b_last - b)).astype(BF16)
            st_ref[h] = st * jnp.exp(b_last) + _dot_tn(val, kt)
            ms = jnp.mean(o * o, axis=-1, keepdims=True)
            o = o * lax.rsqrt(ms + EPS) * nw
            o_ref[rows, cols] = o * _silu(g_ref[rows, cols])


def _hgrn2(proj, lb, norm_w, batch, seq, ct):
    T = proj.shape[0]
    W = HG_HEADS * HG_DK
    nc = seq // ct
    sel, msk = _hg_consts()
    cb = COL_HG // W
    spec = lambda k: pl.BlockSpec((ct, W), lambda b, c: (b * nc + c, cb + k))
    return pl.pallas_call(
        functools.partial(_hg_kernel, nchunk=ct // HG_CHUNK),
        grid=(batch, nc),
        in_specs=[spec(0), spec(1), spec(2), spec(3),
                  pl.BlockSpec((1, W), lambda b, c: (0, 0)),
                  pl.BlockSpec((1, HG_DK), lambda b, c: (0, 0)),
                  pl.BlockSpec(sel.shape, lambda b, c: (0, 0)),
                  pl.BlockSpec(msk.shape, lambda b, c: (0, 0, 0))],
        out_specs=pl.BlockSpec((ct, W), lambda b, c: (b * nc + c, 0)),
        out_shape=jax.ShapeDtypeStruct((T, W), F32),
        scratch_shapes=[pltpu.VMEM((HG_HEADS, HG_DK, HG_DK), F32)],
        compiler_params=_cparams(("parallel", "arbitrary")),
    )(proj, proj, proj, proj, lb.reshape(1, W), norm_w.reshape(1, HG_DK), sel, msk)


SSD_PAD = 8


def _ssd_kernel(z_ref, xbc_ref, dt_ref, cw_ref, cb_ref, dtb_ref, a_ref, d_ref, nw_ref, tril3_ref,
                o_ref, pad_ref, st_ref):
    L = SSD_CHUNK
    W = SSD_WIDTH
    GN = 2 * SSD_STATE

    @pl.when(pl.program_id(1) == 0)
    def _():
        pad_ref[0:SSD_PAD, :] = jnp.zeros((SSD_PAD, SSD_CONV_DIM), F32)
        st_ref[...] = jnp.zeros_like(st_ref)

    pad_ref[SSD_PAD:SSD_PAD + L, :] = xbc_ref[...]
    conv = cb_ref[...]
    for k in range(SSD_CONV):
        off = SSD_PAD - (SSD_CONV - 1) + k
        conv = conv + cw_ref[k:k + 1, :] * pad_ref[off:off + L, :]
    pad_ref[0:SSD_PAD, :] = pad_ref[L:L + SSD_PAD, :]
    xa = _silu(conv)
    xs = xa[:, :W]
    bm = xa[:, W:W + GN]
    cm = xa[:, W + GN:W + 2 * GN]
    dtv = dt_ref[...] + dtb_ref[...]
    dt = jnp.maximum(dtv, 0.0) + jnp.log1p(jnp.exp(-jnp.abs(dtv)))
    a = a_ref[...] * dt
    tril3 = tril3_ref[...]
    a_cum = _dot(tril3, _split3(a, 0))
    a_last = a_cum[L - 1:L, :]
    xdt = xs * dt

    lane = lax.broadcasted_iota(jnp.int32, (1, LANES), 1)
    half = [lane < SSD_STATE, lane >= SSD_STATE]
    row = lax.broadcasted_iota(jnp.int32, (L, L), 0)
    col = lax.broadcasted_iota(jnp.int32, (L, L), 1)
    strict = col < row
    causal = col <= row
    bb = bm.astype(BF16)
    cbm = cm.astype(BF16)
    cb_g = [_dot_nt(jnp.where(half[g], cm, 0.0).astype(BF16), bb) for g in range(2)]

    y_parts = []
    for pair in range(SSD_HEADS // 2):
        g = pair // 2
        xp = xdt[:, pair * LANES:(pair + 1) * LANES]
        acc = jnp.zeros((L, LANES), F32)
        for hh in range(2):
            h = 2 * pair + hh
            a_col = a[:, h * SSD_HEAD_DIM:h * SSD_HEAD_DIM + 1]
            diff = _dot(tril3, _split3(jnp.where(strict, a_col, 0.0), 0))
            seg = jnp.exp(jnp.where(causal, diff, NEG_BIG))
            sc = (cb_g[g] * seg).astype(BF16)
            acc = acc + _dot(sc, jnp.where(half[hh], xp, 0.0).astype(BF16))
        y_parts.append(acc)
    y = jnp.concatenate(y_parts, axis=1)

    st = st_ref[...]
    y = y + jnp.exp(a_cum) * _dot(cbm, st.astype(BF16))
    srow = lax.broadcasted_iota(jnp.int32, (GN, W), 0) // SSD_STATE
    scol = lax.broadcasted_iota(jnp.int32, (GN, W), 1) // (W // 2)
    upd = _dot_tn(bb, (jnp.exp(a_last - a_cum) * xdt).astype(BF16))
    st_ref[...] = st * jnp.exp(a_last) + jnp.where(srow == scol, upd, 0.0)

    y = y + d_ref[...] * xs
    y = y * _silu(z_ref[...])
    nw = nw_ref[...]
    outs = []
    for g in range(2):
        yg = y[:, g * (W // 2):(g + 1) * (W // 2)]
        ms = jnp.mean(yg * yg, axis=-1, keepdims=True)
        outs.append(yg * lax.rsqrt(ms + EPS) * nw[:, g * (W // 2):(g + 1) * (W // 2)])
    o_ref[...] = jnp.concatenate(outs, axis=1)


def _ssd(proj, conv_w, conv_b, dt_bias, a_log, d_skip, norm_w, batch, seq):
    T = proj.shape[0]
    L = SSD_CHUNK
    W = SSD_WIDTH
    nc = seq // L
    rep = lambda v: jnp.repeat(v.astype(F32), SSD_HEAD_DIM).reshape(1, W)
    tril = np.tril(np.ones((L, L), np.float32))
    tril3 = jnp.asarray(np.concatenate([tril, tril, tril], 1), dtype=BF16)
    vec = lambda n: pl.BlockSpec((1, n), lambda b, c: (0, 0))
    return pl.pallas_call(
        _ssd_kernel,
        grid=(batch, nc),
        in_specs=[
            pl.BlockSpec((L, W), lambda b, c: (b * nc + c, COL_Z // W)),
            pl.BlockSpec((L, SSD_CONV_DIM), lambda b, c: (b * nc + c, COL_XBC // SSD_CONV_DIM)),
            pl.BlockSpec((L, W), lambda b, c: (b * nc + c, COL_DT // W)),
            pl.BlockSpec((SSD_CONV, SSD_CONV_DIM), lambda b, c: (0, 0)),
            vec(SSD_CONV_DIM), vec(W), vec(W), vec(W), vec(W),
            pl.BlockSpec((L, 3 * L), lambda b, c: (0, 0)),
        ],
        out_specs=pl.BlockSpec((L, W), lambda b, c: (b * nc + c, 0)),
        out_shape=jax.ShapeDtypeStruct((T, W), F32),
        scratch_shapes=[pltpu.VMEM((L + SSD_PAD, SSD_CONV_DIM), F32),
                        pltpu.VMEM((2 * SSD_STATE, W), F32)],
        compiler_params=_cparams(("parallel", "arbitrary")),
    )(proj, proj, proj, conv_w, conv_b.reshape(1, SSD_CONV_DIM), rep(dt_bias),
      rep(-jnp.exp(a_log.astype(F32))), rep(d_skip), norm_w.reshape(1, W), tril3)


def _merge_kernel(ysb_ref, yhg_ref, yssd_ref, g0_ref, g1_ref, g2_ref, x_ref,
                  w0_ref, w1_ref, w2_ref, wo_ref, nw_ref, xo_ref, h_ref):
    m = _sigmoid_pair(g0_ref[...])[0] * _dot(ysb_ref[...].astype(BF16), w0_ref[...])
    m = m + _sigmoid_pair(g1_ref[...])[0] * _dot(yhg_ref[...].astype(BF16), w1_ref[...])
    m = m + _sigmoid_pair(g2_ref[...])[0] * _dot(yssd_ref[...].astype(BF16), w2_ref[...])
    xn = x_ref[...] + _dot(m.astype(BF16), wo_ref[...])
    xo_ref[...] = xn
    ms = jnp.mean(xn * xn, axis=-1, keepdims=True)
    h_ref[...] = (xn * lax.rsqrt(ms + EPS) * nw_ref[...]).astype(BF16)


def _merge(y_sb, y_hg, y_ssd, proj, x, w_sb, w_hg, w_ssd, w_out, norm_w, tm):
    T, D = x.shape
    Wb = y_sb.shape[1]
    yspec = pl.BlockSpec((tm, Wb), lambda i: (i, 0))
    gspec = lambda k: pl.BlockSpec((tm, D), lambda i: (i, COL_GATE // D + k))
    wspec = pl.BlockSpec((Wb, D), lambda i: (0, 0))
    xspec = pl.BlockSpec((tm, D), lambda i: (i, 0))
    return pl.pallas_call(
        _merge_kernel,
        grid=(T // tm,),
        in_specs=[yspec, yspec, yspec, gspec(0), gspec(1), gspec(2), xspec,
                  wspec, wspec, wspec,
                  pl.BlockSpec((D, D), lambda i: (0, 0)),
                  pl.BlockSpec((1, D), lambda i: (0, 0))],
        out_specs=[xspec, xspec],
        out_shape=[jax.ShapeDtypeStruct((T, D), F32), jax.ShapeDtypeStruct((T, D), BF16)],
        compiler_params=_cparams(("parallel",)),
    )(y_sb, y_hg, y_ssd, proj, proj, proj, x, w_sb, w_hg, w_ssd, w_out, norm_w.reshape(1, D))


def _peerq_kernel(h_ref, wq_ref, sk_ref, s1_ref, s2_ref, ht_ref):
    h = h_ref[...]
    q = _dot(h, wq_ref[...]).astype(BF16)
    for hd in range(PEER_HEADS):
        for p, s_ref in enumerate((s1_ref, s2_ref)):
            j = hd * 2 + p
            s_ref[hd] = _dot_nt(sk_ref[j], q[:, j * PEER_NKEYS:(j + 1) * PEER_NKEYS])
    ht_ref[...] = h.astype(F32).T.astype(BF16)


def _peer_query(h2, w_q, sub_keys, tm):
    T, D = h2.shape
    NK = PEER_NKEYS
    sspec = pl.BlockSpec((PEER_HEADS, NK, tm), lambda i: (0, 0, i))
    return pl.pallas_call(
        _peerq_kernel,
        grid=(T // tm,),
        in_specs=[pl.BlockSpec((tm, D), lambda i: (i, 0)),
                  pl.BlockSpec(w_q.shape, lambda i: (0, 0)),
                  pl.BlockSpec(sub_keys.shape, lambda i: (0, 0, 0))],
        out_specs=[sspec, sspec, pl.BlockSpec((D, tm), lambda i: (0, i))],
        out_shape=[jax.ShapeDtypeStruct((PEER_HEADS, NK, T), F32),
                   jax.ShapeDtypeStruct((PEER_HEADS, NK, T), F32),
                   jax.ShapeDtypeStruct((D, T), BF16)],
        compiler_params=_cparams(("parallel",)),
    )(h2, w_q, sub_keys)


def _topvals(s_ref, v_ref, c_ref):
    x = s_ref[0]
    for r in range(PEER_TOPK):
        m = jnp.max(x, axis=0, keepdims=True)
        eq = x == m
        v_ref[r:r + 1, :] = m
        c_ref[r:r + 1, :] = jnp.sum(jnp.where(eq, 1.0, 0.0), axis=0, keepdims=True)
        x = jnp.where(eq, -jnp.inf, x)


def _peer_topk_kernel(s1_ref, s2_ref, o_ref, v1_ref, c1_ref, v2_ref, c2_ref):
    K = PEER_TOPK
    _topvals(s1_ref, v1_ref, c1_ref)
    _topvals(s2_ref, v2_ref, c2_ref)
    v1 = v1_ref[...]
    c1 = c1_ref[...]
    cands, wts = [], []
    for j in range(K):
        nj = K if j == 0 else 8
        cands.append(v1[:nj] + v2_ref[j:j + 1, :])
        wts.append(c1[:nj] * c2_ref[j:j + 1, :])
    cand = jnp.concatenate(cands, axis=0)
    wt = jnp.concatenate(wts, axis=0)
    m0 = v1[0:1] + v2_ref[0:1, :]
    total = jnp.zeros_like(m0)
    tau = m0
    zsum = jnp.zeros_like(m0)
    for r in range(K):
        m = jnp.max(cand, axis=0, keepdims=True)
        eq = cand == m
        cnt = jnp.sum(jnp.where(eq, wt, 0.0), axis=0, keepdims=True)
        need = total < K
        take = jnp.minimum(cnt, K - total)
        zsum = zsum + jnp.where(need, take * jnp.exp(m - m0), 0.0)
        tau = jnp.where(need, m, tau)
        total = total + cnt
        cand = jnp.where(eq, -jnp.inf, cand)
    o_ref[0, 0:1, :] = tau
    o_ref[0, 1:2, :] = v1[0:1]
    o_ref[0, 2:3, :] = v2_ref[0:1, :]
    o_ref[0, 3:4, :] = 1.0 / zsum


def _peer_topk(s1t, s2t, tt):
    H, NK, T = s1t.shape
    sspec = pl.BlockSpec((1, NK, tt), lambda i, h: (h, 0, i))
    scr = pltpu.VMEM((PEER_TOPK, tt), F32)
    return pl.pallas_call(
        _peer_topk_kernel,
        grid=(T // tt, H),
        in_specs=[sspec, sspec],
        out_specs=pl.BlockSpec((1, 4, tt), lambda i, h: (h, 0, i)),
        out_shape=jax.ShapeDtypeStruct((H, 4, T), F32),
        scratch_shapes=[scr, scr, scr, scr],
        compiler_params=_cparams(("parallel", "parallel")),
    )(s1t, s2t)


def _peer_kernel(x_ref, ht_ref, s1_ref, s2_ref, st_ref, u_ref, vt_ref, fw_ref, o_ref,
                 r1_ref, e2_ref, act_ref, p_ref, acc_ref, *, eb, tt, final):
    e = pl.program_id(1)
    NK = PEER_NKEYS

    @pl.when(e == 0)
    def _():
        for h in range(PEER_HEADS):
            r1_ref[h] = jnp.exp(s1_ref[h] - st_ref[h, 1:2, :]) * st_ref[h, 3:4, :]
            e2_ref[h] = jnp.exp(s2_ref[h] - st_ref[h, 2:3, :])
        acc_ref[...] = jnp.zeros_like(acc_ref)

    act_ref[...] = _dot(u_ref[...], ht_ref[...])

    assert eb // NK == SUBLANES
    a0 = pl.multiple_of(e * SUBLANES, SUBLANES)

    def t_body(tc, _):
        ts = pl.ds(pl.multiple_of(tc * LANES, LANES), LANES)
        for ai in range(SUBLANES):
            w = jnp.zeros((NK, LANES), F32)
            for h in range(PEER_HEADS):
                s1g = s1_ref[h, pl.ds(a0, SUBLANES), ts]
                r1g = r1_ref[h, pl.ds(a0, SUBLANES), ts]
                tot = s2_ref[h, :, ts] + s1g[ai:ai + 1, :]
                gate = e2_ref[h, :, ts] * r1g[ai:ai + 1, :]
                w = w + jnp.where(tot >= st_ref[h, 0:1, ts], gate, 0.0)
            act = act_ref[ai * NK:(ai + 1) * NK, ts]
            gelu = 0.5 * act * (1.0 + lax.erf(act * (2.0 ** -0.5)))
            p_ref[ai * NK:(ai + 1) * NK, ts] = (w * gelu).astype(BF16)
        return 0

    lax.fori_loop(0, tt // LANES, t_body, 0)
    acc_ref[...] += _dot(vt_ref[...], p_ref[...])

    @pl.when(e == pl.num_programs(1) - 1)
    def _():
        y = x_ref[...] + acc_ref[...].T
        if final:
            ms = jnp.mean(y * y, axis=-1, keepdims=True)
            y = y * lax.rsqrt(ms + EPS) * fw_ref[...]
        o_ref[...] = y


def _peer(x, h2t, s1t, s2t, stats, u, vt, final_w, tt, eb, final):
    T, D = x.shape
    E = u.shape[0]
    H, NK = PEER_HEADS, PEER_NKEYS
    sspec = pl.BlockSpec((H, NK, tt), lambda i, e: (0, 0, i))
    return pl.pallas_call(
        functools.partial(_peer_kernel, eb=eb, tt=tt, final=final),
        grid=(T // tt, E // eb),
        in_specs=[pl.BlockSpec((tt, D), lambda i, e: (i, 0)),
                  pl.BlockSpec((D, tt), lambda i, e: (0, i)),
                  sspec, sspec,
                  pl.BlockSpec((H, 4, tt), lambda i, e: (0, 0, i)),
                  pl.BlockSpec((eb, D), lambda i, e: (e, 0)),
                  pl.BlockSpec((D, eb), lambda i, e: (0, e)),
                  pl.BlockSpec((1, D), lambda i, e: (0, 0))],
        out_specs=pl.BlockSpec((tt, D), lambda i, e: (i, 0)),
        out_shape=jax.ShapeDtypeStruct((T, D), F32),
        scratch_shapes=[pltpu.VMEM((H, NK, tt), F32), pltpu.VMEM((H, NK, tt), F32),
                        pltpu.VMEM((eb, tt), F32), pltpu.VMEM((eb, tt), BF16),
                        pltpu.VMEM((D, tt), F32)],
        compiler_params=_cparams(("parallel", "arbitrary")),
    )(x, h2t, s1t, s2t, stats, u, vt, final_w.reshape(1, D))


def _layer(x, batch, seq, lb, p, final_w, final, tiles):
    proj = _inproj(x, p["norm1_w"], p["w_in"], tiles["in_tm"], tiles["in_tn"])
    y_sb = _sb_attention(proj, batch, seq, tiles["sb_blk"], tiles["sb_nsub"])
    y_hg = _hgrn2(proj, lb, p["hg_norm_w"], batch, seq, tiles["hg_ct"])
    y_ssd = _ssd(proj, p["ssd_conv_w"], p["ssd_conv_b"], p["ssd_dt_bias"], p["ssd_a_log"],
                 p["ssd_d"], p["ssd_norm_w"], batch, seq)
    x, h2 = _merge(y_sb, y_hg, y_ssd, proj, x, p["w_branch_sb"], p["w_branch_hg"],
                   p["w_branch_ssd"], p["w_out"], p["norm2_w"], tiles["mg_tm"])
    s1t, s2t, h2t = _peer_query(h2, p["peer_w_q"], p["peer_sub_keys"], tiles["pq_tm"])
    stats = _peer_topk(s1t, s2t, tiles["tk_tt"])
    return _peer(x, h2t, s1t, s2t, stats, p["peer_u"], p["peer_vt"], final_w,
                 tiles["pe_tt"], tiles["pe_eb"], final)


TILES = dict(in_tm=1024, in_tn=768, sb_blk=128, sb_nsub=2, hg_ct=256, mg_tm=256, pq_tm=512,
             tk_tt=256, pe_tt=512, pe_eb=1024)


def _prep_w_in(w):
    c = np.cumsum([0, 512, 512, 512, 512, 512, 512, 512, 512, 768, 8, 1024, 1024, 1024])
    dt_rep = jnp.repeat(w[:, c[9]:c[10]], SSD_HEAD_DIM, axis=1)
    return jnp.concatenate([w[:, c[3]:c[7]], w[:, c[7]:c[8]], dt_rep, w[:, c[10]:c[13]],
                            w[:, c[8]:c[9]], w[:, c[0]:c[3]]], axis=1).astype(BF16)


def _trunk(x, norm1_w, w_in, hg_lb_logits, hg_norm_w, ssd_conv_w, ssd_conv_b, ssd_dt_bias,
           ssd_a_log, ssd_d, ssd_norm_w, w_branch_sb, w_branch_hg, w_branch_ssd, w_out,
           norm2_w, peer_w_q, peer_sub_keys, peer_u, peer_v, final_norm_w, tiles):
    batch, seq, d = x.shape
    depth = w_in.shape[0]
    gamma = jax.nn.softmax(hg_lb_logits.astype(F32), axis=0)
    lower_bounds = jnp.cumsum(gamma, axis=0) - gamma[0]
    xt = x.reshape(batch * seq, d)
    for l in range(depth):
        p = dict(
            norm1_w=norm1_w[l], w_in=_prep_w_in(w_in[l]), hg_norm_w=hg_norm_w[l],
            ssd_conv_w=ssd_conv_w[l], ssd_conv_b=ssd_conv_b[l], ssd_dt_bias=ssd_dt_bias[l],
            ssd_a_log=ssd_a_log[l], ssd_d=ssd_d[l], ssd_norm_w=ssd_norm_w[l],
            w_branch_sb=w_branch_sb[l].astype(BF16), w_branch_hg=w_branch_hg[l].astype(BF16),
            w_branch_ssd=w_branch_ssd[l].astype(BF16), w_out=w_out[l].astype(BF16),
            norm2_w=norm2_w[l], peer_w_q=peer_w_q[l].astype(BF16),
            peer_sub_keys=peer_sub_keys[l].reshape(2 * PEER_HEADS, PEER_NKEYS, -1).astype(BF16),
            peer_u=peer_u[l].astype(BF16), peer_vt=peer_v[l].T.astype(BF16))
        xt = _layer(xt, batch, seq, lower_bounds[l], p, final_norm_w, l == depth - 1, tiles)
    return xt.reshape(batch, seq, d)


def kernel(x, norm1_w, w_in, hg_lb_logits, hg_norm_w, ssd_conv_w, ssd_conv_b, ssd_dt_bias, ssd_a_log, ssd_d, ssd_norm_w, w_branch_sb, w_branch_hg, w_branch_ssd, w_out, norm2_w, peer_w_q, peer_sub_keys, peer_u, peer_v, final_norm_w):
    return _trunk(x, norm1_w, w_in, hg_lb_logits, hg_norm_w, ssd_conv_w, ssd_conv_b, ssd_dt_bias,
                  ssd_a_log, ssd_d, ssd_norm_w, w_branch_sb, w_branch_hg, w_branch_ssd, w_out,
                  norm2_w, peer_w_q, peer_sub_keys, peer_u, peer_v, final_norm_w, TILES)
```

```python
import functools

import numpy as np
import jax
import jax.numpy as jnp
from jax import lax
from jax.experimental import pallas as pl
from jax.experimental.pallas import tpu as pltpu

F32 = jnp.float32
BF16 = jnp.bfloat16

D_MODEL = 1024
SB_HEAD_DIM = 64
SB_WIDTH = 512
HG_HEADS = 4
HG_DK = 128
HG_CHUNK = 64
HG_F_FLOOR = 1e-30
SSD_HEADS = 8
SSD_HEAD_DIM = 64
SSD_WIDTH = 512
SSD_STATE = 64
SSD_CONV = 4
SSD_CHUNK = 128
SSD_CONV_DIM = 768
PEER_HEADS = 8
PEER_NKEYS = 128
PEER_TOPK = 16
EPS = 1e-6
NEG_BIG = -1e30

LANES = 128
SUBLANES = 8
VMEM_LIMIT = 56 * 1024 * 1024

COL_HG = 0
COL_Z = 2048
COL_DT = 2560
COL_GATE = 3072
COL_XBC = 6144
COL_SB = 6912
IN_COLS_PADDED = 8448


def _cparams(sem):
    return pltpu.CompilerParams(dimension_semantics=sem, vmem_limit_bytes=VMEM_LIMIT)


def _dot(a, b):
    return jnp.dot(a, b, preferred_element_type=F32)


def _dot_nt(a, b):
    return lax.dot_general(a, b, (((1,), (1,)), ((), ())), preferred_element_type=F32)


def _dot_tn(a, b):
    return lax.dot_general(a, b, (((0,), (0,)), ((), ())), preferred_element_type=F32)


def _sigmoid_pair(x):
    e = jnp.exp(-jnp.abs(x))
    inv = 1.0 / (1.0 + e)
    small = e * inv
    pos = x >= 0
    return jnp.where(pos, inv, small), jnp.where(pos, small, inv)


def _silu(x):
    return x * _sigmoid_pair(x)[0]


def _split3(x, axis):
    hi = x.astype(BF16)
    r1 = x - hi.astype(F32)
    mid = r1.astype(BF16)
    lo = (r1 - mid.astype(F32)).astype(BF16)
    return jnp.concatenate([hi, mid, lo], axis=axis)


def _inproj_kernel(x_ref, nw_ref, w_ref, o_ref, h_ref):
    @pl.when(pl.program_id(1) == 0)
    def _():
        x = x_ref[...]
        ms = jnp.mean(x * x, axis=-1, keepdims=True)
        h_ref[...] = (x * lax.rsqrt(ms + EPS) * nw_ref[...]).astype(BF16)

    o_ref[...] = _dot(h_ref[...], w_ref[...])


def _inproj(x, norm_w, w, tm, tn):
    T, D = x.shape
    N = w.shape[1]
    return pl.pallas_call(
        _inproj_kernel,
        grid=(T // tm, N // tn),
        in_specs=[
            pl.BlockSpec((tm, D), lambda i, j: (i, 0)),
            pl.BlockSpec((1, D), lambda i, j: (0, 0)),
            pl.BlockSpec((D, tn), lambda i, j: (0, j)),
        ],
        out_specs=pl.BlockSpec((tm, tn), lambda i, j: (i, j)),
        out_shape=jax.ShapeDtypeStruct((T, N), F32),
        scratch_shapes=[pltpu.VMEM((tm, D), BF16)],
        compiler_params=_cparams(("parallel", "arbitrary")),
    )(x, norm_w.reshape(1, D), w)


SB_EXIT = -110.0


def _sb_kernel(q_ref, k_ref, v_ref, mcat_ref, o_ref, c_ref, acc_ref, *, blk, nsub):
    qi = pl.program_id(2)
    lane = lax.broadcasted_iota(jnp.int32, (1, LANES), 1)
    first_head = lane < SB_HEAD_DIM
    mcat = mcat_ref[...]
    qparts = []
    for s in range(nsub):
        q = q_ref[s * blk:(s + 1) * blk, :] * (SB_HEAD_DIM ** -0.5)
        qparts += [jnp.where(first_head, q, 0.0), jnp.where(first_head, 0.0, q)]
    qs = jnp.concatenate(qparts, axis=0).astype(BF16)
    c_ref[...] = jnp.zeros_like(c_ref)
    acc_ref[...] = jnp.zeros_like(acc_ref)

    def visit(kb, s0, diag):
        r0 = s0 * 2 * blk
        ks = pl.multiple_of(kb * blk, blk)
        kblk = k_ref[pl.ds(ks, blk), :].astype(BF16)
        vblk = v_ref[pl.ds(ks, blk), :].astype(BF16)
        z = _dot_nt(qs[r0:], kblk)
        t = jnp.log1p(jnp.exp(-jnp.abs(z)))
        log_beta = jnp.minimum(z, 0.0) - t
        log_rest = jnp.minimum(-z, 0.0) - t
        if diag:
            row = lax.broadcasted_iota(jnp.int32, z.shape, 0)
            col = lax.broadcasted_iota(jnp.int32, z.shape, 1)
            before = (row >= 2 * blk) | (col < (row & (blk - 1)))
            log_rest = jnp.where(before, log_rest, 0.0)
        hi = log_rest.astype(BF16)
        lo = (log_rest - hi.astype(F32)).astype(BF16)
        cs = _dot(jnp.concatenate([hi, lo], axis=1), mcat)
        carry = c_ref[r0:, :]
        a = jnp.exp(log_beta + cs[:, :blk] + carry)
        if diag:
            a = jnp.where(before, a, 0.0)
        carry = carry + cs[:, blk:]
        c_ref[r0:, :] = carry
        av = _dot(a.astype(BF16), vblk)
        for s in range(s0, nsub):
            i = (s - s0) * 2 * blk
            acc_ref[s * blk:(s + 1) * blk, :] += jnp.where(first_head, av[i:i + blk], av[i + blk:i + 2 * blk])
        return carry

    for j in range(nsub - 1, -1, -1):
        carry = visit(qi * nsub + j, j, True)

    def body(state):
        kb, _ = state
        return kb - 1, jnp.max(visit(kb, 0, False))

    lax.while_loop(lambda st: (st[0] >= 0) & (st[1] > SB_EXIT), body,
                   (qi * nsub - 1, jnp.max(carry)))
    o_ref[...] = acc_ref[...]


def _sb_consts(blk):
    r = np.arange(2 * blk)[:, None] % blk
    c = np.arange(2 * blk)[None, :]
    return jnp.asarray(((c >= blk) | (r > c)).astype(np.float32), dtype=BF16)


def _sb_attention(proj, batch, seq, blk, nsub):
    T = proj.shape[0]
    tq = blk * nsub
    nq = seq // tq
    cb = COL_SB // LANES
    npair = SB_WIDTH // LANES
    return pl.pallas_call(
        functools.partial(_sb_kernel, blk=blk, nsub=nsub),
        grid=(batch, npair, nq),
        in_specs=[
            pl.BlockSpec((tq, LANES), lambda b, p, i: (b * nq + i, cb + p)),
            pl.BlockSpec((seq, LANES), lambda b, p, i: (b, cb + npair + p)),
            pl.BlockSpec((seq, LANES), lambda b, p, i: (b, cb + 2 * npair + p)),
            pl.BlockSpec((2 * blk, 2 * blk), lambda b, p, i: (0, 0)),
        ],
        out_specs=pl.BlockSpec((tq, LANES), lambda b, p, i: (b * nq + i, p)),
        out_shape=jax.ShapeDtypeStruct((T, SB_WIDTH), F32),
        scratch_shapes=[pltpu.VMEM((nsub * 2 * blk, blk), F32), pltpu.VMEM((tq, LANES), F32)],
        compiler_params=_cparams(("parallel", "parallel", "arbitrary")),
    )(proj, proj, proj, _sb_consts(blk))


HG_LEVELS = (32, 16, 8, 4, 2, 1)


def _hg_consts():
    C = HG_CHUNK
    i = np.arange(C)
    sels = [i[None, :] <= i[:, None]]
    masks = []
    for m in HG_LEVELS:
        g = (i // (2 * m)) * (2 * m) + m - 1
        sels.append(i[None, :] <= g[:, None])
        same = (i[:, None] // (2 * m)) == (i[None, :] // (2 * m))
        upper_t = (i[:, None] % (2 * m)) >= m
        lower_s = (i[None, :] % (2 * m)) < m
        masks.append(same & upper_t & lower_s)
    masks.append(np.eye(C, dtype=bool))
    big = np.concatenate(sels, 0).astype(np.float32)
    big3 = np.concatenate([big, big, big], 1)
    return jnp.asarray(big3, dtype=BF16), jnp.asarray(np.stack(masks).astype(np.float32))


def _hg_kernel(f_ref, q_ref, i_ref, g_ref, lb_ref, nw_ref, sel_ref, msk_ref, o_ref, st_ref, *, nchunk):
    C = HG_CHUNK

    @pl.when(pl.program_id(1) == 0)
    def _():
        st_ref[...] = jnp.zeros_like(st_ref)

    sel = sel_ref[...]
    nw = nw_ref[...]
    for c in range(nchunk):
        rows = slice(c * C, (c + 1) * C)
        for h in range(HG_HEADS):
            cols = slice(h * HG_DK, (h + 1) * HG_DK)
            lb = lb_ref[:, cols]
            sp, sn = _sigmoid_pair(f_ref[rows, cols])
            f = lb + (1.0 - lb) * sp
            log_f = jnp.log(jnp.maximum(f, HG_F_FLOOR))
            key = (1.0 - lb) * sn
            q = _silu(q_ref[rows, cols]) * (HG_DK ** -0.5)
            val = i_ref[rows, cols].astype(BF16)
            br = _dot(sel, _split3(log_f, 0))
            b = br[:C]
            s = msk_ref[len(HG_LEVELS)] * _dot_nt(q.astype(BF16), key.astype(BF16))
            for lvl in range(len(HG_LEVELS)):
                r = br[(lvl + 1) * C:(lvl + 2) * C]
                qm = (q * jnp.exp(jnp.minimum(b - r, 0.0))).astype(BF16)
                km = (key * jnp.exp(jnp.minimum(r - b, 0.0))).astype(BF16)
                s = s + msk_ref[lvl] * _dot_nt(qm, km)
            st = st_ref[h]
            o = _dot(s.astype(BF16), val)
            o = o + _dot_nt((q * jnp.exp(b)).astype(BF16), st.astype(BF16))
            b_last = b[C - 1:C, :]
            kt = (key * jnp.exp(b_last - b)).astype(BF16)
            st_ref[h] = st * jnp.exp(b_last) + _dot_tn(val, kt)
            ms = jnp.mean(o * o, axis=-1, keepdims=True)
            o = o * lax.rsqrt(ms + EPS) * nw
            o_ref[rows, cols] = o * _silu(g_ref[rows, cols])


def _hgrn2(proj, lb, norm_w, batch, seq, ct):
    T = proj.shape[0]
    W = HG_HEADS * HG_DK
    nc = seq // ct
    sel, msk = _hg_consts()
    cb = COL_HG // W
    spec = lambda k: pl.BlockSpec((ct, W), lambda b, c: (b * nc + c, cb + k))
    return pl.pallas_call(
        functools.partial(_hg_kernel, nchunk=ct // HG_CHUNK),
        grid=(batch, nc),
        in_specs=[spec(0), spec(1), spec(2), spec(3),
                  pl.BlockSpec((1, W), lambda b, c: (0, 0)),
                  pl.BlockSpec((1, HG_DK), lambda b, c: (0, 0)),
                  pl.BlockSpec(sel.shape, lambda b, c: (0, 0)),
                  pl.BlockSpec(msk.shape, lambda b, c: (0, 0, 0))],
        out_specs=pl.BlockSpec((ct, W), lambda b, c: (b * nc + c, 0)),
        out_shape=jax.ShapeDtypeStruct((T, W), F32),
        scratch_shapes=[pltpu.VMEM((HG_HEADS, HG_DK, HG_DK), F32)],
        compiler_params=_cparams(("parallel", "arbitrary")),
    )(proj, proj, proj, proj, lb.reshape(1, W), norm_w.reshape(1, HG_DK), sel, msk)


SSD_PAD = 8


def _ssd_kernel(z_ref, xbc_ref, dt_ref, cw_ref, cb_ref, dtb_ref, a_ref, d_ref, nw_ref, tril3_ref,
                o_ref, pad_ref, st_ref):
    L = SSD_CHUNK
    W = SSD_WIDTH
    GN = 2 * SSD_STATE

    @pl.when(pl.program_id(1) == 0)
    def _():
        pad_ref[0:SSD_PAD, :] = jnp.zeros((SSD_PAD, SSD_CONV_DIM), F32)
        st_ref[...] = jnp.zeros_like(st_ref)

    pad_ref[SSD_PAD:SSD_PAD + L, :] = xbc_ref[...]
    conv = cb_ref[...]
    for k in range(SSD_CONV):
        off = SSD_PAD - (SSD_CONV - 1) + k
        conv = conv + cw_ref[k:k + 1, :] * pad_ref[off:off + L, :]
    pad_ref[0:SSD_PAD, :] = pad_ref[L:L + SSD_PAD, :]
    xa = _silu(conv)
    xs = xa[:, :W]
    bm = xa[:, W:W + GN]
    cm = xa[:, W + GN:W + 2 * GN]
    dtv = dt_ref[...] + dtb_ref[...]
    dt = jnp.maximum(dtv, 0.0) + jnp.log1p(jnp.exp(-jnp.abs(dtv)))
    a = a_ref[...] * dt
    tril3 = tril3_ref[...]
    a_cum = _dot(tril3, _split3(a, 0))
    a_last = a_cum[L - 1:L, :]
    xdt = xs * dt

    lane = lax.broadcasted_iota(jnp.int32, (1, LANES), 1)
    half = [lane < SSD_STATE, lane >= SSD_STATE]
    row = lax.broadcasted_iota(jnp.int32, (L, L), 0)
    col = lax.broadcasted_iota(jnp.int32, (L, L), 1)
    strict = col < row
    causal = col <= row
    bb = bm.astype(BF16)
    cbm = cm.astype(BF16)
    cb_g = [_dot_nt(jnp.where(half[g], cm, 0.0).astype(BF16), bb) for g in range(2)]

    y_parts = []
    for pair in range(SSD_HEADS // 2):
        g = pair // 2
        xp = xdt[:, pair * LANES:(pair + 1) * LANES]
        acc = jnp.zeros((L, LANES), F32)
        for hh in range(2):
            h = 2 * pair + hh
            a_col = a[:, h * SSD_HEAD_DIM:h * SSD_HEAD_DIM + 1]
            diff = _dot(tril3, _split3(jnp.where(strict, a_col, 0.0), 0))
            seg = jnp.exp(jnp.where(causal, diff, NEG_BIG))
            sc = (cb_g[g] * seg).astype(BF16)
            acc = acc + _dot(sc, jnp.where(half[hh], xp, 0.0).astype(BF16))
        y_parts.append(acc)
    y = jnp.concatenate(y_parts, axis=1)

    st = st_ref[...]
    y = y + jnp.exp(a_cum) * _dot(cbm, st.astype(BF16))
    srow = lax.broadcasted_iota(jnp.int32, (GN, W), 0) // SSD_STATE
    scol = lax.broadcasted_iota(jnp.int32, (GN, W), 1) // (W // 2)
    upd = _dot_tn(bb, (jnp.exp(a_last - a_cum) * xdt).astype(BF16))
    st_ref[...] = st * jnp.exp(a_last) + jnp.where(srow == scol, upd, 0.0)

    y = y + d_ref[...] * xs
    y = y * _silu(z_ref[...])
    nw = nw_ref[...]
    outs = []
    for g in range(2):
        yg = y[:, g * (W // 2):(g + 1) * (W // 2)]
        ms = jnp.mean(yg * yg, axis=-1, keepdims=True)
        outs.append(yg * lax.rsqrt(ms + EPS) * nw[:, g * (W // 2):(g + 1) * (W // 2)])
    o_ref[...] = jnp.concatenate(outs, axis=1)


def _ssd(proj, conv_w, conv_b, dt_bias, a_log, d_skip, norm_w, batch, seq):
    T = proj.shape[0]
    L = SSD_CHUNK
    W = SSD_WIDTH
    nc = seq // L
    rep = lambda v: jnp.repeat(v.astype(F32), SSD_HEAD_DIM).reshape(1, W)
    tril = np.tril(np.ones((L, L), np.float32))
    tril3 = jnp.asarray(np.concatenate([tril, tril, tril], 1), dtype=BF16)
    vec = lambda n: pl.BlockSpec((1, n), lambda b, c: (0, 0))
    return pl.pallas_call(
        _ssd_kernel,
        grid=(batch, nc),
        in_specs=[
            pl.BlockSpec((L, W), lambda b, c: (b * nc + c, COL_Z // W)),
            pl.BlockSpec((L, SSD_CONV_DIM), lambda b, c: (b * nc + c, COL_XBC // SSD_CONV_DIM)),
            pl.BlockSpec((L, W), lambda b, c: (b * nc + c, COL_DT // W)),
            pl.BlockSpec((SSD_CONV, SSD_CONV_DIM), lambda b, c: (0, 0)),
            vec(SSD_CONV_DIM), vec(W), vec(W), vec(W), vec(W),
            pl.BlockSpec((L, 3 * L), lambda b, c: (0, 0)),
        ],
        out_specs=pl.BlockSpec((L, W), lambda b, c: (b * nc + c, 0)),
        out_shape=jax.ShapeDtypeStruct((T, W), F32),
        scratch_shapes=[pltpu.VMEM((L + SSD_PAD, SSD_CONV_DIM), F32),
                        pltpu.VMEM((2 * SSD_STATE, W), F32)],
        compiler_params=_cparams(("parallel", "arbitrary")),
    )(proj, proj, proj, conv_w, conv_b.reshape(1, SSD_CONV_DIM), rep(dt_bias),
      rep(-jnp.exp(a_log.astype(F32))), rep(d_skip), norm_w.reshape(1, W), tril3)


def _merge_kernel(ysb_ref, yhg_ref, yssd_ref, g0_ref, g1_ref, g2_ref, x_ref,
                  w0_ref, w1_ref, w2_ref, wo_ref, nw_ref, xo_ref, h_ref):
    m = _sigmoid_pair(g0_ref[...])[0] * _dot(ysb_ref[...].astype(BF16), w0_ref[...])
    m = m + _sigmoid_pair(g1_ref[...])[0] * _dot(yhg_ref[...].astype(BF16), w1_ref[...])
    m = m + _sigmoid_pair(g2_ref[...])[0] * _dot(yssd_ref[...].astype(BF16), w2_ref[...])
    xn = x_ref[...] + _dot(m.astype(BF16), wo_ref[...])
    xo_ref[...] = xn
    ms = jnp.mean(xn * xn, axis=-1, keepdims=True)
    h_ref[...] = (xn * lax.rsqrt(ms + EPS) * nw_ref[...]).astype(BF16)


def _merge(y_sb, y_hg, y_ssd, proj, x, w_sb, w_hg, w_ssd, w_out, norm_w, tm):
    T, D = x.shape
    Wb = y_sb.shape[1]
    yspec = pl.BlockSpec((tm, Wb), lambda i: (i, 0))
    gspec = lambda k: pl.BlockSpec((tm, D), lambda i: (i, COL_GATE // D + k))
    wspec = pl.BlockSpec((Wb, D), lambda i: (0, 0))
    xspec = pl.BlockSpec((tm, D), lambda i: (i, 0))
    return pl.pallas_call(
        _merge_kernel,
        grid=(T // tm,),
        in_specs=[yspec, yspec, yspec, gspec(0), gspec(1), gspec(2), xspec,
                  wspec, wspec, wspec,
                  pl.BlockSpec((D, D), lambda i: (0, 0)),
                  pl.BlockSpec((1, D), lambda i: (0, 0))],
        out_specs=[xspec, xspec],
        out_shape=[jax.ShapeDtypeStruct((T, D), F32), jax.ShapeDtypeStruct((T, D), BF16)],
        compiler_params=_cparams(("parallel",)),
    )(y_sb, y_hg, y_ssd, proj, proj, proj, x, w_sb, w_hg, w_ssd, w_out, norm_w.reshape(1, D))


def _peerq_kernel(h_ref, wq_ref, sk_ref, s1_ref, s2_ref, ht_ref):
    h = h_ref[...]
    q = _dot(h, wq_ref[...]).astype(BF16)
    for hd in range(PEER_HEADS):
        for p, s_ref in enumerate((s1_ref, s2_ref)):
            j = hd * 2 + p
            s_ref[hd] = _dot_nt(sk_ref[j], q[:, j * PEER_NKEYS:(j + 1) * PEER_NKEYS])
    ht_ref[...] = h.astype(F32).T.astype(BF16)


def _peer_query(h2, w_q, sub_keys, tm):
    T, D = h2.shape
    NK = PEER_NKEYS
    sspec = pl.BlockSpec((PEER_HEADS, NK, tm), lambda i: (0, 0, i))
    return pl.pallas_call(
        _peerq_kernel,
        grid=(T // tm,),
        in_specs=[pl.BlockSpec((tm, D), lambda i: (i, 0)),
                  pl.BlockSpec(w_q.shape, lambda i: (0, 0)),
                  pl.BlockSpec(sub_keys.shape, lambda i: (0, 0, 0))],
        out_specs=[sspec, sspec, pl.BlockSpec((D, tm), lambda i: (0, i))],
        out_shape=[jax.ShapeDtypeStruct((PEER_HEADS, NK, T), F32),
                   jax.ShapeDtypeStruct((PEER_HEADS, NK, T), F32),
                   jax.ShapeDtypeStruct((D, T), BF16)],
        compiler_params=_cparams(("parallel",)),
    )(h2, w_q, sub_keys)


def _oddeven_merge(lo, hi, r):
    step = r * 2
    if step < hi - lo:
        yield from _oddeven_merge(lo, hi, step)
        yield from _oddeven_merge(lo + r, hi, step)
        yield from [(i, i + r) for i in range(lo + r, hi - r, step)]
    else:
        yield (lo, lo + r)


def _oddeven_sort(lo, hi):
    if hi - lo >= 1:
        mid = lo + (hi - lo) // 2
        yield from _oddeven_sort(lo, mid)
        yield from _oddeven_sort(mid + 1, hi)
        yield from _oddeven_merge(lo, hi, 1)


SORT16 = tuple(_oddeven_sort(0, PEER_TOPK - 1))


def _cmp_exchange(x, i, j):
    x[i], x[j] = jnp.maximum(x[i], x[j]), jnp.minimum(x[i], x[j])


def _top16(x, n_valid):
    K = PEER_TOPK
    for i, j in SORT16:
        if j < n_valid:
            _cmp_exchange(x, i, j)
    for shift in (4, 2, 1):
        y = [pltpu.roll(v, shift, 0) for v in x]
        x = [jnp.maximum(x[i], y[K - 1 - i]) for i in range(K)]
        d = K // 2
        while d >= 1:
            for i in range(K):
                if i & d == 0:
                    _cmp_exchange(x, i, i + d)
            d //= 2
    return x


def _pack_rows(rep, sub):
    out = rep[0]
    for i in range(1, SUBLANES):
        out = jnp.where(sub == i, rep[i], out)
    return out


def _peer_topk_kernel(s1_ref, s2_ref, rank_ref, e2_ref, n_ref, r1_ref):
    K = PEER_TOPK
    G = PEER_NKEYS // SUBLANES
    tt = s1_ref.shape[2]
    s1 = [s1_ref[0, g * SUBLANES:(g + 1) * SUBLANES, :] for g in range(G)]
    s2 = [s2_ref[0, g * SUBLANES:(g + 1) * SUBLANES, :] for g in range(G)]
    v1 = _top16(list(s1), G)
    v2 = _top16(list(s2), G)
    sub = lax.broadcasted_iota(jnp.int32, (SUBLANES, tt), 0)
    v1p = [_pack_rows(v1[:SUBLANES], sub), _pack_rows(v1[SUBLANES:], sub)]
    v2p_hi = _pack_rows(v2[SUBLANES:], sub)
    cand = [v1p[0] + v2[0], v1p[1] + v2[0]] + [v1p[0] + v2[j] for j in range(1, SUBLANES)]
    cand.append(v1[0] + v2p_hi)
    ninf = jnp.full((SUBLANES, tt), -jnp.inf, F32)
    top = _top16(cand + [ninf] * (K - len(cand)), len(cand))
    tau = top[K - 1]
    zsum = jnp.ones_like(tau)
    for r in range(1, K):
        zsum = zsum + jnp.exp(top[r] - top[0])
    cnt = [jnp.zeros((SUBLANES, tt), F32), jnp.zeros((SUBLANES, tt), F32)]
    for j in range(K):
        for k in range(2):
            cnt[k] = cnt[k] + jnp.where(v1p[k] + v2[j] >= tau, 1.0, 0.0)
    n_rep = [jnp.broadcast_to(cnt[i // SUBLANES][i % SUBLANES:i % SUBLANES + 1, :], (SUBLANES, tt))
             for i in range(K)]
    n_rep = [n_rep[0]] + [jnp.where(v1[i] < v1[i - 1], n_rep[i], 0.0) for i in range(1, K)]
    inv_z = 1.0 / zsum
    for g in range(G):
        rows = slice(g * SUBLANES, (g + 1) * SUBLANES)
        n = jnp.zeros((SUBLANES, tt), F32)
        for i in range(K - 1, -1, -1):
            n = jnp.where(s1[g] == v1[i], n_rep[i], n)
        n_ref[0, rows, :] = n
        r1_ref[0, rows, :] = jnp.exp(s1[g] - v1[0]) * inv_z
    for g in range(0, G, 2):
        rk, e2 = [], []
        for gg in (g, g + 1):
            r = jnp.zeros((SUBLANES, tt), F32)
            for j in range(K):
                r = r + jnp.where(v2[j] > s2[gg], 1.0, 0.0)
            rk.append(r)
            e2.append(jnp.exp(s2[gg] - v2[0]))
        rows = slice(g * SUBLANES, (g + 2) * SUBLANES)
        rank_ref[0, rows, :] = jnp.concatenate(rk, axis=0).astype(BF16)
        e2_ref[0, rows, :] = jnp.concatenate(e2, axis=0).astype(BF16)


def _peer_topk(s1t, s2t, tt):
    H, NK, T = s1t.shape
    sspec = pl.BlockSpec((1, NK, tt), lambda i, h: (h, 0, i))
    return pl.pallas_call(
        _peer_topk_kernel,
        grid=(T // tt, H),
        in_specs=[sspec, sspec],
        out_specs=[sspec, sspec, sspec, sspec],
        out_shape=[jax.ShapeDtypeStruct((H, NK, T), BF16), jax.ShapeDtypeStruct((H, NK, T), BF16),
                   jax.ShapeDtypeStruct((H, NK, T), F32), jax.ShapeDtypeStruct((H, NK, T), F32)],
        compiler_params=_cparams(("parallel", "parallel")),
    )(s1t, s2t)


def _peer_kernel(x_ref, ht_ref, rank_ref, e2_ref, n_ref, r1_ref, u0_ref, ua_ref, ub_ref, va_ref, vb_ref,
                 vl_ref, fw_ref, o_ref, act_a, act_b, p_a, p_b, acc_ref, *, eb, tt, final):
    s = pl.program_id(1)
    NK = PEER_NKEYS
    assert eb // NK == SUBLANES
    zero = jnp.zeros((), BF16)

    def build_p(blk, act_ref, p_ref):
        a0 = pl.multiple_of(blk * SUBLANES, SUBLANES)
        for tc in range(tt // LANES):
            ts = slice(tc * LANES, (tc + 1) * LANES)
            n8 = [n_ref[h, pl.ds(a0, SUBLANES), ts] for h in range(PEER_HEADS)]
            r8 = [r1_ref[h, pl.ds(a0, SUBLANES), ts] for h in range(PEER_HEADS)]
            for ai in range(SUBLANES):
                w = None
                for h in range(PEER_HEADS):
                    n_row = n8[h][ai:ai + 1, :].astype(BF16)
                    r_row = r8[h][ai:ai + 1, :].astype(BF16)
                    term = jnp.where(rank_ref[h, :, ts] < n_row, e2_ref[h, :, ts] * r_row, zero)
                    w = term if w is None else w + term
                act = act_ref[ai * NK:(ai + 1) * NK, ts]
                gelu = 0.5 * act * (1.0 + lax.erf(act * (2.0 ** -0.5)))
                p_ref[ai * NK:(ai + 1) * NK, ts] = w * gelu.astype(BF16)

    @pl.when(s == 0)
    def _():
        acc_ref[...] = jnp.zeros_like(acc_ref)
        p_b[...] = jnp.zeros_like(p_b)
        act_a[...] = _dot(u0_ref[...], ht_ref[...])

    acc_ref[...] += _dot(va_ref[...], p_b[...])
    act_b[...] = _dot(ua_ref[...], ht_ref[...])
    build_p(2 * s, act_a, p_a)
    acc_ref[...] += _dot(vb_ref[...], p_a[...])
    act_a[...] = _dot(ub_ref[...], ht_ref[...])
    build_p(2 * s + 1, act_b, p_b)

    @pl.when(s == pl.num_programs(1) - 1)
    def _():
        y = x_ref[...] + (acc_ref[...] + _dot(vl_ref[...], p_b[...])).T
        if final:
            ms = jnp.mean(y * y, axis=-1, keepdims=True)
            y = y * lax.rsqrt(ms + EPS) * fw_ref[...]
        o_ref[...] = y


def _peer(x, h2t, rank2, e2, n1, r1, u, vt, final_w, tt, eb, final):
    T, D = x.shape
    E = u.shape[0]
    H, NK = PEER_HEADS, PEER_NKEYS
    ne = E // eb
    sspec = pl.BlockSpec((H, NK, tt), lambda i, s: (0, 0, i))
    once = pl.Buffered(1)
    return pl.pallas_call(
        functools.partial(_peer_kernel, eb=eb, tt=tt, final=final),
        grid=(T // tt, ne // 2),
        in_specs=[pl.BlockSpec((tt, D), lambda i, s: (i, 0)),
                  pl.BlockSpec((D, tt), lambda i, s: (0, i)),
                  sspec, sspec, sspec, sspec,
                  pl.BlockSpec((eb, D), lambda i, s: (0, 0), pipeline_mode=once),
                  pl.BlockSpec((eb, D), lambda i, s: (2 * s + 1, 0)),
                  pl.BlockSpec((eb, D), lambda i, s: (jnp.minimum(2 * s + 2, ne - 1), 0)),
                  pl.BlockSpec((D, eb), lambda i, s: (0, jnp.maximum(2 * s - 1, 0))),
                  pl.BlockSpec((D, eb), lambda i, s: (0, 2 * s)),
                  pl.BlockSpec((D, eb), lambda i, s: (0, ne - 1), pipeline_mode=once),
                  pl.BlockSpec((1, D), lambda i, s: (0, 0))],
        out_specs=pl.BlockSpec((tt, D), lambda i, s: (i, 0)),
        out_shape=jax.ShapeDtypeStruct((T, D), F32),
        scratch_shapes=[pltpu.VMEM((eb, tt), F32), pltpu.VMEM((eb, tt), F32),
                        pltpu.VMEM((eb, tt), BF16), pltpu.VMEM((eb, tt), BF16),
                        pltpu.VMEM((D, tt), F32)],
        compiler_params=_cparams(("parallel", "arbitrary")),
    )(x, h2t, rank2, e2, n1, r1, u, u, u, vt, vt, vt, final_w.reshape(1, D))


def _layer(x, batch, seq, lb, p, final_w, final, tiles):
    proj = _inproj(x, p["norm1_w"], p["w_in"], tiles["in_tm"], tiles["in_tn"])
    y_sb = _sb_attention(proj, batch, seq, tiles["sb_blk"], tiles["sb_nsub"])
    y_hg = _hgrn2(proj, lb, p["hg_norm_w"], batch, seq, tiles["hg_ct"])
    y_ssd = _ssd(proj, p["ssd_conv_w"], p["ssd_conv_b"], p["ssd_dt_bias"], p["ssd_a_log"],
                 p["ssd_d"], p["ssd_norm_w"], batch, seq)
    x, h2 = _merge(y_sb, y_hg, y_ssd, proj, x, p["w_branch_sb"], p["w_branch_hg"],
                   p["w_branch_ssd"], p["w_out"], p["norm2_w"], tiles["mg_tm"])
    s1t, s2t, h2t = _peer_query(h2, p["peer_w_q"], p["peer_sub_keys"], tiles["pq_tm"])
    rank2, e2, n1, r1 = _peer_topk(s1t, s2t, tiles["tk_tt"])
    return _peer(x, h2t, rank2, e2, n1, r1, p["peer_u"], p["peer_vt"], final_w,
                 tiles["pe_tt"], tiles["pe_eb"], final)


TILES = dict(in_tm=1024, in_tn=768, sb_blk=128, sb_nsub=2, hg_ct=256, mg_tm=256, pq_tm=512,
             tk_tt=256, pe_tt=512, pe_eb=1024)


def _prep_w_in(w):
    c = np.cumsum([0, 512, 512, 512, 512, 512, 512, 512, 512, 768, 8, 1024, 1024, 1024])
    dt_rep = jnp.repeat(w[:, c[9]:c[10]], SSD_HEAD_DIM, axis=1)
    return jnp.concatenate([w[:, c[3]:c[7]], w[:, c[7]:c[8]], dt_rep, w[:, c[10]:c[13]],
                            w[:, c[8]:c[9]], w[:, c[0]:c[3]]], axis=1).astype(BF16)


def _trunk(x, norm1_w, w_in, hg_lb_logits, hg_norm_w, ssd_conv_w, ssd_conv_b, ssd_dt_bias,
           ssd_a_log, ssd_d, ssd_norm_w, w_branch_sb, w_branch_hg, w_branch_ssd, w_out,
           norm2_w, peer_w_q, peer_sub_keys, peer_u, peer_v, final_norm_w, tiles):
    batch, seq, d = x.shape
    depth = w_in.shape[0]
    gamma = jax.nn.softmax(hg_lb_logits.astype(F32), axis=0)
    lower_bounds = jnp.cumsum(gamma, axis=0) - gamma[0]
    xt = x.reshape(batch * seq, d)
    for l in range(depth):
        p = dict(
            norm1_w=norm1_w[l], w_in=_prep_w_in(w_in[l]), hg_norm_w=hg_norm_w[l],
            ssd_conv_w=ssd_conv_w[l], ssd_conv_b=ssd_conv_b[l], ssd_dt_bias=ssd_dt_bias[l],
            ssd_a_log=ssd_a_log[l], ssd_d=ssd_d[l], ssd_norm_w=ssd_norm_w[l],
            w_branch_sb=w_branch_sb[l].astype(BF16), w_branch_hg=w_branch_hg[l].astype(BF16),
            w_branch_ssd=w_branch_ssd[l].astype(BF16), w_out=w_out[l].astype(BF16),
            norm2_w=norm2_w[l], peer_w_q=peer_w_q[l].astype(BF16),
            peer_sub_keys=peer_sub_keys[l].reshape(2 * PEER_HEADS, PEER_NKEYS, -1).astype(BF16),
            peer_u=peer_u[l].astype(BF16), peer_vt=peer_v[l].T.astype(BF16))
        xt = _layer(xt, batch, seq, lower_bounds[l], p, final_norm_w, l == depth - 1, tiles)
    return xt.reshape(batch, seq, d)


def kernel(x, norm1_w, w_in, hg_lb_logits, hg_norm_w, ssd_conv_w, ssd_conv_b, ssd_dt_bias, ssd_a_log, ssd_d, ssd_norm_w, w_branch_sb, w_branch_hg, w_branch_ssd, w_out, norm2_w, peer_w_q, peer_sub_keys, peer_u, peer_v, final_norm_w):
    return _trunk(x, norm1_w, w_in, hg_lb_logits, hg_norm_w, ssd_conv_w, ssd_conv_b, ssd_dt_bias,
                  ssd_a_log, ssd_d, ssd_norm_w, w_branch_sb, w_branch_hg, w_branch_ssd, w_out,
                  norm2_w, peer_w_q, peer_sub_keys, peer_u, peer_v, final_norm_w, TILES)
```

```python
import functools

import numpy as np
import jax
import jax.numpy as jnp
from jax import lax
from jax.experimental import pallas as pl
from jax.experimental.pallas import tpu as pltpu

F32 = jnp.float32
BF16 = jnp.bfloat16

D_MODEL = 1024
SB_HEAD_DIM = 64
SB_WIDTH = 512
HG_HEADS = 4
HG_DK = 128
HG_CHUNK = 64
HG_F_FLOOR = 1e-30
SSD_HEADS = 8
SSD_HEAD_DIM = 64
SSD_WIDTH = 512
SSD_STATE = 64
SSD_CONV = 4
SSD_CHUNK = 128
SSD_CONV_DIM = 768
PEER_HEADS = 8
PEER_NKEYS = 128
PEER_TOPK = 16
EPS = 1e-6
NEG_BIG = -1e30

LANES = 128
SUBLANES = 8
VMEM_LIMIT = 56 * 1024 * 1024

COL_HG = 0
COL_Z = 2048
COL_DT = 2560
COL_GATE = 3072
COL_XBC = 6144
COL_SB = 6912
IN_COLS_PADDED = 8448


def _cparams(sem):
    return pltpu.CompilerParams(dimension_semantics=sem, vmem_limit_bytes=VMEM_LIMIT)


def _dot(a, b):
    return jnp.dot(a, b, preferred_element_type=F32)


def _dot_nt(a, b):
    return lax.dot_general(a, b, (((1,), (1,)), ((), ())), preferred_element_type=F32)


def _dot_tn(a, b):
    return lax.dot_general(a, b, (((0,), (0,)), ((), ())), preferred_element_type=F32)


def _sigmoid_pair(x):
    e = jnp.exp(-jnp.abs(x))
    inv = 1.0 / (1.0 + e)
    small = e * inv
    pos = x >= 0
    return jnp.where(pos, inv, small), jnp.where(pos, small, inv)


def _silu(x):
    return x * _sigmoid_pair(x)[0]


def _split3(x, axis):
    hi = x.astype(BF16)
    r1 = x - hi.astype(F32)
    mid = r1.astype(BF16)
    lo = (r1 - mid.astype(F32)).astype(BF16)
    return jnp.concatenate([hi, mid, lo], axis=axis)


def _inproj_kernel(x_ref, nw_ref, w_ref, o_ref, h_ref):
    @pl.when(pl.program_id(1) == 0)
    def _():
        x = x_ref[...]
        ms = jnp.mean(x * x, axis=-1, keepdims=True)
        h_ref[...] = (x * lax.rsqrt(ms + EPS) * nw_ref[...]).astype(BF16)

    o_ref[...] = _dot(h_ref[...], w_ref[...])


def _inproj(x, norm_w, w, tm, tn):
    T, D = x.shape
    N = w.shape[1]
    return pl.pallas_call(
        _inproj_kernel,
        grid=(T // tm, N // tn),
        in_specs=[
            pl.BlockSpec((tm, D), lambda i, j: (i, 0)),
            pl.BlockSpec((1, D), lambda i, j: (0, 0)),
            pl.BlockSpec((D, tn), lambda i, j: (0, j)),
        ],
        out_specs=pl.BlockSpec((tm, tn), lambda i, j: (i, j)),
        out_shape=jax.ShapeDtypeStruct((T, N), F32),
        scratch_shapes=[pltpu.VMEM((tm, D), BF16)],
        compiler_params=_cparams(("parallel", "arbitrary")),
    )(x, norm_w.reshape(1, D), w)


SB_EXIT = -110.0


def _sb_kernel(q_ref, k_ref, v_ref, mcat_ref, o_ref, c_ref, acc_ref, *, blk, nsub):
    qi = pl.program_id(2)
    lane = lax.broadcasted_iota(jnp.int32, (1, LANES), 1)
    first_head = lane < SB_HEAD_DIM
    mcat = mcat_ref[...]
    qparts = []
    for s in range(nsub):
        q = q_ref[s * blk:(s + 1) * blk, :] * (SB_HEAD_DIM ** -0.5)
        qparts += [jnp.where(first_head, q, 0.0), jnp.where(first_head, 0.0, q)]
    qs = jnp.concatenate(qparts, axis=0).astype(BF16)
    c_ref[...] = jnp.zeros_like(c_ref)
    acc_ref[...] = jnp.zeros_like(acc_ref)

    def visit(kb, s0, diag):
        r0 = s0 * 2 * blk
        ks = pl.multiple_of(kb * blk, blk)
        kblk = k_ref[pl.ds(ks, blk), :].astype(BF16)
        vblk = v_ref[pl.ds(ks, blk), :].astype(BF16)
        z = _dot_nt(qs[r0:], kblk)
        t = jnp.log1p(jnp.exp(-jnp.abs(z)))
        log_beta = jnp.minimum(z, 0.0) - t
        log_rest = jnp.minimum(-z, 0.0) - t
        if diag:
            row = lax.broadcasted_iota(jnp.int32, z.shape, 0)
            col = lax.broadcasted_iota(jnp.int32, z.shape, 1)
            before = (row >= 2 * blk) | (col < (row & (blk - 1)))
            log_rest = jnp.where(before, log_rest, 0.0)
        hi = log_rest.astype(BF16)
        lo = (log_rest - hi.astype(F32)).astype(BF16)
        cs = _dot(jnp.concatenate([hi, lo], axis=1), mcat)
        carry = c_ref[r0:, :]
        a = jnp.exp(log_beta + cs[:, :blk] + carry)
        if diag:
            a = jnp.where(before, a, 0.0)
        carry = carry + cs[:, blk:]
        c_ref[r0:, :] = carry
        av = _dot(a.astype(BF16), vblk)
        for s in range(s0, nsub):
            i = (s - s0) * 2 * blk
            acc_ref[s * blk:(s + 1) * blk, :] += jnp.where(first_head, av[i:i + blk], av[i + blk:i + 2 * blk])
        return carry

    for j in range(nsub - 1, -1, -1):
        carry = visit(qi * nsub + j, j, True)

    def body(state):
        kb, _ = state
        return kb - 1, jnp.max(visit(kb, 0, False))

    lax.while_loop(lambda st: (st[0] >= 0) & (st[1] > SB_EXIT), body,
                   (qi * nsub - 1, jnp.max(carry)))
    o_ref[...] = acc_ref[...]


def _sb_consts(blk):
    r = np.arange(2 * blk)[:, None] % blk
    c = np.arange(2 * blk)[None, :]
    return jnp.asarray(((c >= blk) | (r > c)).astype(np.float32), dtype=BF16)


def _sb_attention(proj, batch, seq, blk, nsub):
    T = proj.shape[0]
    tq = blk * nsub
    nq = seq // tq
    cb = COL_SB // LANES
    npair = SB_WIDTH // LANES
    return pl.pallas_call(
        functools.partial(_sb_kernel, blk=blk, nsub=nsub),
        grid=(batch, npair, nq),
        in_specs=[
            pl.BlockSpec((tq, LANES), lambda b, p, i: (b * nq + i, cb + p)),
            pl.BlockSpec((seq, LANES), lambda b, p, i: (b, cb + npair + p)),
            pl.BlockSpec((seq, LANES), lambda b, p, i: (b, cb + 2 * npair + p)),
            pl.BlockSpec((2 * blk, 2 * blk), lambda b, p, i: (0, 0)),
        ],
        out_specs=pl.BlockSpec((tq, LANES), lambda b, p, i: (b * nq + i, p)),
        out_shape=jax.ShapeDtypeStruct((T, SB_WIDTH), F32),
        scratch_shapes=[pltpu.VMEM((nsub * 2 * blk, blk), F32), pltpu.VMEM((tq, LANES), F32)],
        compiler_params=_cparams(("parallel", "parallel", "arbitrary")),
    )(proj, proj, proj, _sb_consts(blk))


HG_LEVELS = (32, 16, 8, 4, 2, 1)


def _hg_consts():
    C = HG_CHUNK
    i = np.arange(C)
    sels = [i[None, :] <= i[:, None]]
    masks = []
    for m in HG_LEVELS:
        g = (i // (2 * m)) * (2 * m) + m - 1
        sels.append(i[None, :] <= g[:, None])
        same = (i[:, None] // (2 * m)) == (i[None, :] // (2 * m))
        upper_t = (i[:, None] % (2 * m)) >= m
        lower_s = (i[None, :] % (2 * m)) < m
        masks.append(same & upper_t & lower_s)
    masks.append(np.eye(C, dtype=bool))
    big = np.concatenate(sels, 0).astype(np.float32)
    big3 = np.concatenate([big, big, big], 1)
    return jnp.asarray(big3, dtype=BF16), jnp.asarray(np.stack(masks).astype(np.float32))


def _hg_kernel(f_ref, q_ref, i_ref, g_ref, lb_ref, nw_ref, sel_ref, msk_ref, o_ref, st_ref, *, nchunk):
    C = HG_CHUNK

    @pl.when(pl.program_id(1) == 0)
    def _():
        st_ref[...] = jnp.zeros_like(st_ref)

    sel = sel_ref[...]
    nw = nw_ref[...]
    for c in range(nchunk):
        rows = slice(c * C, (c + 1) * C)
        for h in range(HG_HEADS):
            cols = slice(h * HG_DK, (h + 1) * HG_DK)
            lb = lb_ref[:, cols]
            sp, sn = _sigmoid_pair(f_ref[rows, cols])
            f = lb + (1.0 - lb) * sp
            log_f = jnp.log(jnp.maximum(f, HG_F_FLOOR))
            key = (1.0 - lb) * sn
            q = _silu(q_ref[rows, cols]) * (HG_DK ** -0.5)
            val = i_ref[rows, cols].astype(BF16)
            br = _dot(sel, _split3(log_f, 0))
            b = br[:C]
            s = msk_ref[len(HG_LEVELS)] * _dot_nt(q.astype(BF16), key.astype(BF16))
            for lvl in range(len(HG_LEVELS)):
                r = br[(lvl + 1) * C:(lvl + 2) * C]
                qm = (q * jnp.exp(jnp.minimum(b - r, 0.0))).astype(BF16)
                km = (key * jnp.exp(jnp.minimum(r - b, 0.0))).astype(BF16)
                s = s + msk_ref[lvl] * _dot_nt(qm, km)
            st = st_ref[h]
            o = _dot(s.astype(BF16), val)
            o = o + _dot_nt((q * jnp.exp(b)).astype(BF16), st.astype(BF16))
            b_last = b[C - 1:C, :]
            kt = (key * jnp.exp(b_last - b)).astype(BF16)
            st_ref[h] = st * jnp.exp(b_last) + _dot_tn(val, kt)
            ms = jnp.mean(o * o, axis=-1, keepdims=True)
            o = o * lax.rsqrt(ms + EPS) * nw
            o_ref[rows, cols] = o * _silu(g_ref[rows, cols])


def _hgrn2(proj, lb, norm_w, batch, seq, ct):
    T = proj.shape[0]
    W = HG_HEADS * HG_DK
    nc = seq // ct
    sel, msk = _hg_consts()
    cb = COL_HG // W
    spec = lambda k: pl.BlockSpec((ct, W), lambda b, c: (b * nc + c, cb + k))
    return pl.pallas_call(
        functools.partial(_hg_kernel, nchunk=ct // HG_CHUNK),
        grid=(batch, nc),
        in_specs=[spec(0), spec(1), spec(2), spec(3),
                  pl.BlockSpec((1, W), lambda b, c: (0, 0)),
                  pl.BlockSpec((1, HG_DK), lambda b, c: (0, 0)),
                  pl.BlockSpec(sel.shape, lambda b, c: (0, 0)),
                  pl.BlockSpec(msk.shape, lambda b, c: (0, 0, 0))],
        out_specs=pl.BlockSpec((ct, W), lambda b, c: (b * nc + c, 0)),
        out_shape=jax.ShapeDtypeStruct((T, W), F32),
        scratch_shapes=[pltpu.VMEM((HG_HEADS, HG_DK, HG_DK), F32)],
        compiler_params=_cparams(("parallel", "arbitrary")),
    )(proj, proj, proj, proj, lb.reshape(1, W), norm_w.reshape(1, HG_DK), sel, msk)


SSD_PAD = 8


def _ssd_kernel(z_ref, xbc_ref, dt_ref, cw_ref, cb_ref, dtb_ref, a_ref, d_ref, nw_ref, tril3_ref,
                o_ref, pad_ref, st_ref):
    L = SSD_CHUNK
    W = SSD_WIDTH
    GN = 2 * SSD_STATE

    @pl.when(pl.program_id(1) == 0)
    def _():
        pad_ref[0:SSD_PAD, :] = jnp.zeros((SSD_PAD, SSD_CONV_DIM), F32)
        st_ref[...] = jnp.zeros_like(st_ref)

    pad_ref[SSD_PAD:SSD_PAD + L, :] = xbc_ref[...]
    conv = cb_ref[...]
    for k in range(SSD_CONV):
        off = SSD_PAD - (SSD_CONV - 1) + k
        conv = conv + cw_ref[k:k + 1, :] * pad_ref[off:off + L, :]
    pad_ref[0:SSD_PAD, :] = pad_ref[L:L + SSD_PAD, :]
    xa = _silu(conv)
    xs = xa[:, :W]
    bm = xa[:, W:W + GN]
    cm = xa[:, W + GN:W + 2 * GN]
    dtv = dt_ref[...] + dtb_ref[...]
    dt = jnp.maximum(dtv, 0.0) + jnp.log1p(jnp.exp(-jnp.abs(dtv)))
    a = a_ref[...] * dt
    tril3 = tril3_ref[...]
    a_cum = _dot(tril3, _split3(a, 0))
    a_last = a_cum[L - 1:L, :]
    xdt = xs * dt

    lane = lax.broadcasted_iota(jnp.int32, (1, LANES), 1)
    half = [lane < SSD_STATE, lane >= SSD_STATE]
    row = lax.broadcasted_iota(jnp.int32, (L, L), 0)
    col = lax.broadcasted_iota(jnp.int32, (L, L), 1)
    strict = col < row
    causal = col <= row
    bb = bm.astype(BF16)
    cbm = cm.astype(BF16)
    cb_g = [_dot_nt(jnp.where(half[g], cm, 0.0).astype(BF16), bb) for g in range(2)]

    y_parts = []
    for pair in range(SSD_HEADS // 2):
        g = pair // 2
        xp = xdt[:, pair * LANES:(pair + 1) * LANES]
        acc = jnp.zeros((L, LANES), F32)
        for hh in range(2):
            h = 2 * pair + hh
            a_col = a[:, h * SSD_HEAD_DIM:h * SSD_HEAD_DIM + 1]
            diff = _dot(tril3, _split3(jnp.where(strict, a_col, 0.0), 0))
            seg = jnp.exp(jnp.where(causal, diff, NEG_BIG))
            sc = (cb_g[g] * seg).astype(BF16)
            acc = acc + _dot(sc, jnp.where(half[hh], xp, 0.0).astype(BF16))
        y_parts.append(acc)
    y = jnp.concatenate(y_parts, axis=1)

    st = st_ref[...]
    y = y + jnp.exp(a_cum) * _dot(cbm, st.astype(BF16))
    srow = lax.broadcasted_iota(jnp.int32, (GN, W), 0) // SSD_STATE
    scol = lax.broadcasted_iota(jnp.int32, (GN, W), 1) // (W // 2)
    upd = _dot_tn(bb, (jnp.exp(a_last - a_cum) * xdt).astype(BF16))
    st_ref[...] = st * jnp.exp(a_last) + jnp.where(srow == scol, upd, 0.0)

    y = y + d_ref[...] * xs
    y = y * _silu(z_ref[...])
    nw = nw_ref[...]
    outs = []
    for g in range(2):
        yg = y[:, g * (W // 2):(g + 1) * (W // 2)]
        ms = jnp.mean(yg * yg, axis=-1, keepdims=True)
        outs.append(yg * lax.rsqrt(ms + EPS) * nw[:, g * (W // 2):(g + 1) * (W // 2)])
    o_ref[...] = jnp.concatenate(outs, axis=1)


def _ssd(proj, conv_w, conv_b, dt_bias, a_log, d_skip, norm_w, batch, seq):
    T = proj.shape[0]
    L = SSD_CHUNK
    W = SSD_WIDTH
    nc = seq // L
    rep = lambda v: jnp.repeat(v.astype(F32), SSD_HEAD_DIM).reshape(1, W)
    tril = np.tril(np.ones((L, L), np.float32))
    tril3 = jnp.asarray(np.concatenate([tril, tril, tril], 1), dtype=BF16)
    vec = lambda n: pl.BlockSpec((1, n), lambda b, c: (0, 0))
    return pl.pallas_call(
        _ssd_kernel,
        grid=(batch, nc),
        in_specs=[
            pl.BlockSpec((L, W), lambda b, c: (b * nc + c, COL_Z // W)),
            pl.BlockSpec((L, SSD_CONV_DIM), lambda b, c: (b * nc + c, COL_XBC // SSD_CONV_DIM)),
            pl.BlockSpec((L, W), lambda b, c: (b * nc + c, COL_DT // W)),
            pl.BlockSpec((SSD_CONV, SSD_CONV_DIM), lambda b, c: (0, 0)),
            vec(SSD_CONV_DIM), vec(W), vec(W), vec(W), vec(W),
            pl.BlockSpec((L, 3 * L), lambda b, c: (0, 0)),
        ],
        out_specs=pl.BlockSpec((L, W), lambda b, c: (b * nc + c, 0)),
        out_shape=jax.ShapeDtypeStruct((T, W), F32),
        scratch_shapes=[pltpu.VMEM((L + SSD_PAD, SSD_CONV_DIM), F32),
                        pltpu.VMEM((2 * SSD_STATE, W), F32)],
        compiler_params=_cparams(("parallel", "arbitrary")),
    )(proj, proj, proj, conv_w, conv_b.reshape(1, SSD_CONV_DIM), rep(dt_bias),
      rep(-jnp.exp(a_log.astype(F32))), rep(d_skip), norm_w.reshape(1, W), tril3)


def _merge_kernel(ysb_ref, yhg_ref, yssd_ref, g0_ref, g1_ref, g2_ref, x_ref,
                  w0_ref, w1_ref, w2_ref, wo_ref, nw_ref, xo_ref, h_ref):
    m = _sigmoid_pair(g0_ref[...])[0] * _dot(ysb_ref[...].astype(BF16), w0_ref[...])
    m = m + _sigmoid_pair(g1_ref[...])[0] * _dot(yhg_ref[...].astype(BF16), w1_ref[...])
    m = m + _sigmoid_pair(g2_ref[...])[0] * _dot(yssd_ref[...].astype(BF16), w2_ref[...])
    xn = x_ref[...] + _dot(m.astype(BF16), wo_ref[...])
    xo_ref[...] = xn
    ms = jnp.mean(xn * xn, axis=-1, keepdims=True)
    h_ref[...] = (xn * lax.rsqrt(ms + EPS) * nw_ref[...]).astype(BF16)


def _merge(y_sb, y_hg, y_ssd, proj, x, w_sb, w_hg, w_ssd, w_out, norm_w, tm):
    T, D = x.shape
    Wb = y_sb.shape[1]
    yspec = pl.BlockSpec((tm, Wb), lambda i: (i, 0))
    gspec = lambda k: pl.BlockSpec((tm, D), lambda i: (i, COL_GATE // D + k))
    wspec = pl.BlockSpec((Wb, D), lambda i: (0, 0))
    xspec = pl.BlockSpec((tm, D), lambda i: (i, 0))
    return pl.pallas_call(
        _merge_kernel,
        grid=(T // tm,),
        in_specs=[yspec, yspec, yspec, gspec(0), gspec(1), gspec(2), xspec,
                  wspec, wspec, wspec,
                  pl.BlockSpec((D, D), lambda i: (0, 0)),
                  pl.BlockSpec((1, D), lambda i: (0, 0))],
        out_specs=[xspec, xspec],
        out_shape=[jax.ShapeDtypeStruct((T, D), F32), jax.ShapeDtypeStruct((T, D), BF16)],
        compiler_params=_cparams(("parallel",)),
    )(y_sb, y_hg, y_ssd, proj, proj, proj, x, w_sb, w_hg, w_ssd, w_out, norm_w.reshape(1, D))


def _peerq_kernel(h_ref, wq_ref, sk_ref, s1_ref, s2_ref, ht_ref):
    h = h_ref[...]
    q = _dot(h, wq_ref[...]).astype(BF16)
    for hd in range(PEER_HEADS):
        for p, s_ref in enumerate((s1_ref, s2_ref)):
            j = hd * 2 + p
            s_ref[hd] = _dot_nt(sk_ref[j], q[:, j * PEER_NKEYS:(j + 1) * PEER_NKEYS])
    ht_ref[...] = h.astype(F32).T.astype(BF16)


def _peer_query(h2, w_q, sub_keys, tm):
    T, D = h2.shape
    NK = PEER_NKEYS
    sspec = pl.BlockSpec((PEER_HEADS, NK, tm), lambda i: (0, 0, i))
    return pl.pallas_call(
        _peerq_kernel,
        grid=(T // tm,),
        in_specs=[pl.BlockSpec((tm, D), lambda i: (i, 0)),
                  pl.BlockSpec(w_q.shape, lambda i: (0, 0)),
                  pl.BlockSpec(sub_keys.shape, lambda i: (0, 0, 0))],
        out_specs=[sspec, sspec, pl.BlockSpec((None, D, tm), lambda i: (i, 0, 0))],
        out_shape=[jax.ShapeDtypeStruct((PEER_HEADS, NK, T), F32),
                   jax.ShapeDtypeStruct((PEER_HEADS, NK, T), F32),
                   jax.ShapeDtypeStruct((T // tm, D, tm), BF16)],
        compiler_params=_cparams(("parallel",)),
    )(h2, w_q, sub_keys)


def _oddeven_merge(lo, hi, r):
    step = r * 2
    if step < hi - lo:
        yield from _oddeven_merge(lo, hi, step)
        yield from _oddeven_merge(lo + r, hi, step)
        yield from [(i, i + r) for i in range(lo + r, hi - r, step)]
    else:
        yield (lo, lo + r)


def _oddeven_sort(lo, hi):
    if hi - lo >= 1:
        mid = lo + (hi - lo) // 2
        yield from _oddeven_sort(lo, mid)
        yield from _oddeven_sort(mid + 1, hi)
        yield from _oddeven_merge(lo, hi, 1)


SORT16 = tuple(_oddeven_sort(0, PEER_TOPK - 1))


def _cmp_exchange(x, i, j):
    x[i], x[j] = jnp.maximum(x[i], x[j]), jnp.minimum(x[i], x[j])


def _top16(x, n_valid):
    K = PEER_TOPK
    for i, j in SORT16:
        if j < n_valid:
            _cmp_exchange(x, i, j)
    for shift in (4, 2, 1):
        y = [pltpu.roll(v, shift, 0) for v in x]
        x = [jnp.maximum(x[i], y[K - 1 - i]) for i in range(K)]
        d = K // 2
        while d >= 1:
            for i in range(K):
                if i & d == 0:
                    _cmp_exchange(x, i, i + d)
            d //= 2
    return x


def _pack_rows(rep, sub):
    out = rep[0]
    for i in range(1, SUBLANES):
        out = jnp.where(sub == i, rep[i], out)
    return out


def _peer_topk_kernel(s1_ref, s2_ref, rank_ref, e2_ref, n_ref, r1_ref):
    for hd in range(s1_ref.shape[0]):
        _peer_topk_head(hd, s1_ref, s2_ref, rank_ref, e2_ref, n_ref, r1_ref)


def _peer_topk_head(hd, s1_ref, s2_ref, rank_ref, e2_ref, n_ref, r1_ref):
    K = PEER_TOPK
    G = PEER_NKEYS // SUBLANES
    tt = s1_ref.shape[2]
    s1 = [s1_ref[hd, g * SUBLANES:(g + 1) * SUBLANES, :] for g in range(G)]
    s2 = [s2_ref[hd, g * SUBLANES:(g + 1) * SUBLANES, :] for g in range(G)]
    v1 = _top16(list(s1), G)
    v2 = _top16(list(s2), G)
    sub = lax.broadcasted_iota(jnp.int32, (SUBLANES, tt), 0)
    v1p = [_pack_rows(v1[:SUBLANES], sub), _pack_rows(v1[SUBLANES:], sub)]
    v2p_hi = _pack_rows(v2[SUBLANES:], sub)
    cand = [v1p[0] + v2[0], v1p[1] + v2[0]] + [v1p[0] + v2[j] for j in range(1, SUBLANES)]
    cand.append(v1[0] + v2p_hi)
    ninf = jnp.full((SUBLANES, tt), -jnp.inf, F32)
    top = _top16(cand + [ninf] * (K - len(cand)), len(cand))
    tau = top[K - 1]
    zsum = jnp.ones_like(tau)
    for r in range(1, K):
        zsum = zsum + jnp.exp(top[r] - top[0])
    cnt = [jnp.zeros((SUBLANES, tt), F32), jnp.zeros((SUBLANES, tt), F32)]
    for j in range(K):
        for k in range(2):
            cnt[k] = cnt[k] + jnp.where(v1p[k] + v2[j] >= tau, 1.0, 0.0)
    n_rep = [jnp.broadcast_to(cnt[i // SUBLANES][i % SUBLANES:i % SUBLANES + 1, :], (SUBLANES, tt))
             for i in range(K)]
    n_rep = [n_rep[0]] + [jnp.where(v1[i] < v1[i - 1], n_rep[i], 0.0) for i in range(1, K)]
    inv_z = 1.0 / zsum
    for g in range(G):
        rows = slice(g * SUBLANES, (g + 1) * SUBLANES)
        n = jnp.zeros((SUBLANES, tt), F32)
        for i in range(K - 1, -1, -1):
            n = jnp.where(s1[g] == v1[i], n_rep[i], n)
        n_ref[hd, rows, :] = n
        r1_ref[hd, rows, :] = jnp.exp(s1[g] - v1[0]) * inv_z
    for g in range(0, G, 2):
        rk, e2 = [], []
        for gg in (g, g + 1):
            r = jnp.zeros((SUBLANES, tt), F32)
            for j in range(K):
                r = r + jnp.where(v2[j] > s2[gg], 1.0, 0.0)
            rk.append(r)
            e2.append(jnp.exp(s2[gg] - v2[0]))
        rows = slice(g * SUBLANES, (g + 2) * SUBLANES)
        rank_ref[hd, rows, :] = jnp.concatenate(rk, axis=0).astype(BF16)
        e2_ref[hd, rows, :] = jnp.concatenate(e2, axis=0).astype(BF16)


def _peer_topk(s1t, s2t, tt, hb):
    H, NK, T = s1t.shape
    sspec = pl.BlockSpec((hb, NK, tt), lambda i, h: (h, 0, i))
    return pl.pallas_call(
        _peer_topk_kernel,
        grid=(T // tt, H // hb),
        in_specs=[sspec, sspec],
        out_specs=[sspec, sspec, sspec, sspec],
        out_shape=[jax.ShapeDtypeStruct((H, NK, T), BF16), jax.ShapeDtypeStruct((H, NK, T), BF16),
                   jax.ShapeDtypeStruct((H, NK, T), F32), jax.ShapeDtypeStruct((H, NK, T), F32)],
        compiler_params=_cparams(("parallel", "parallel")),
    )(s1t, s2t)


def _peer_kernel(x_ref, ht_ref, rank_ref, e2_ref, n_ref, r1_ref, u0_ref, ua_ref, ub_ref, va_ref, vb_ref,
                 vl_ref, fw_ref, o_ref, act_a, act_b, p_a, p_b, acc_ref, *, eb, tt, final):
    s = pl.program_id(1)
    NK = PEER_NKEYS
    assert eb // NK == SUBLANES
    zero = jnp.zeros((), BF16)

    def build_p(blk, act_ref, p_ref):
        a0 = pl.multiple_of(blk * SUBLANES, SUBLANES)
        for tc in range(tt // LANES):
            ts = slice(tc * LANES, (tc + 1) * LANES)
            n8 = [n_ref[h, pl.ds(a0, SUBLANES), ts] for h in range(PEER_HEADS)]
            r8 = [r1_ref[h, pl.ds(a0, SUBLANES), ts] for h in range(PEER_HEADS)]
            for ai in range(SUBLANES):
                w = None
                for h in range(PEER_HEADS):
                    n_row = n8[h][ai:ai + 1, :].astype(BF16)
                    r_row = r8[h][ai:ai + 1, :].astype(BF16)
                    term = jnp.where(rank_ref[h, :, ts] < n_row, e2_ref[h, :, ts] * r_row, zero)
                    w = term if w is None else w + term
                act = act_ref[ai * NK:(ai + 1) * NK, ts]
                gelu = 0.5 * act * (1.0 + lax.erf(act * (2.0 ** -0.5)))
                p_ref[ai * NK:(ai + 1) * NK, ts] = w * gelu.astype(BF16)

    @pl.when(s == 0)
    def _():
        acc_ref[...] = jnp.zeros_like(acc_ref)
        p_b[...] = jnp.zeros_like(p_b)
        act_a[...] = _dot(u0_ref[...], ht_ref[...])

    acc_ref[...] += _dot(va_ref[...], p_b[...])
    act_b[...] = _dot(ua_ref[...], ht_ref[...])
    build_p(2 * s, act_a, p_a)
    acc_ref[...] += _dot(vb_ref[...], p_a[...])
    act_a[...] = _dot(ub_ref[...], ht_ref[...])
    build_p(2 * s + 1, act_b, p_b)

    @pl.when(s == pl.num_programs(1) - 1)
    def _():
        y = x_ref[...] + (acc_ref[...] + _dot(vl_ref[...], p_b[...])).T
        if final:
            ms = jnp.mean(y * y, axis=-1, keepdims=True)
            y = y * lax.rsqrt(ms + EPS) * fw_ref[...]
        o_ref[...] = y


def _peer(x, h2t, rank2, e2, n1, r1, u, vt, final_w, tt, eb, final):
    T, D = x.shape
    H, NK = PEER_HEADS, PEER_NKEYS
    ne = vt.shape[0]
    assert h2t.shape == (T // tt, D, tt) and vt.shape == (ne, D, eb) and u.shape == (ne * eb, D)
    sspec = pl.BlockSpec((H, NK, tt), lambda i, s: (0, 0, i))
    once = pl.Buffered(1)
    return pl.pallas_call(
        functools.partial(_peer_kernel, eb=eb, tt=tt, final=final),
        grid=(T // tt, ne // 2),
        in_specs=[pl.BlockSpec((tt, D), lambda i, s: (i, 0)),
                  pl.BlockSpec((None, D, tt), lambda i, s: (i, 0, 0)),
                  sspec, sspec, sspec, sspec,
                  pl.BlockSpec((eb, D), lambda i, s: (0, 0), pipeline_mode=once),
                  pl.BlockSpec((eb, D), lambda i, s: (2 * s + 1, 0)),
                  pl.BlockSpec((eb, D), lambda i, s: (jnp.minimum(2 * s + 2, ne - 1), 0)),
                  pl.BlockSpec((None, D, eb), lambda i, s: (jnp.maximum(2 * s - 1, 0), 0, 0)),
                  pl.BlockSpec((None, D, eb), lambda i, s: (2 * s, 0, 0)),
                  pl.BlockSpec((None, D, eb), lambda i, s: (ne - 1, 0, 0), pipeline_mode=once),
                  pl.BlockSpec((1, D), lambda i, s: (0, 0))],
        out_specs=pl.BlockSpec((tt, D), lambda i, s: (i, 0)),
        out_shape=jax.ShapeDtypeStruct((T, D), F32),
        scratch_shapes=[pltpu.VMEM((eb, tt), F32), pltpu.VMEM((eb, tt), F32),
                        pltpu.VMEM((eb, tt), BF16), pltpu.VMEM((eb, tt), BF16),
                        pltpu.VMEM((D, tt), F32)],
        compiler_params=_cparams(("parallel", "arbitrary")),
    )(x, h2t, rank2, e2, n1, r1, u, u, u, vt, vt, vt, final_w.reshape(1, D))


def _layer(x, batch, seq, lb, p, final_w, final, tiles):
    proj = _inproj(x, p["norm1_w"], p["w_in"], tiles["in_tm"], tiles["in_tn"])
    y_sb = _sb_attention(proj, batch, seq, tiles["sb_blk"], tiles["sb_nsub"])
    y_hg = _hgrn2(proj, lb, p["hg_norm_w"], batch, seq, tiles["hg_ct"])
    y_ssd = _ssd(proj, p["ssd_conv_w"], p["ssd_conv_b"], p["ssd_dt_bias"], p["ssd_a_log"],
                 p["ssd_d"], p["ssd_norm_w"], batch, seq)
    x, h2 = _merge(y_sb, y_hg, y_ssd, proj, x, p["w_branch_sb"], p["w_branch_hg"],
                   p["w_branch_ssd"], p["w_out"], p["norm2_w"], tiles["mg_tm"])
    s1t, s2t, h2t = _peer_query(h2, p["peer_w_q"], p["peer_sub_keys"], tiles["pq_tm"])
    rank2, e2, n1, r1 = _peer_topk(s1t, s2t, tiles["tk_tt"], tiles["tk_hb"])
    return _peer(x, h2t, rank2, e2, n1, r1, p["peer_u"], p["peer_vt"], final_w,
                 tiles["pe_tt"], tiles["pe_eb"], final)


TILES = dict(in_tm=1024, in_tn=768, sb_blk=128, sb_nsub=2, hg_ct=256, mg_tm=256, pq_tm=512,
             tk_tt=256, tk_hb=2, pe_tt=512, pe_eb=1024)


def _prep_w_in(w):
    c = np.cumsum([0, 512, 512, 512, 512, 512, 512, 512, 512, 768, 8, 1024, 1024, 1024])
    dt_rep = jnp.repeat(w[:, c[9]:c[10]], SSD_HEAD_DIM, axis=1)
    return jnp.concatenate([w[:, c[3]:c[7]], w[:, c[7]:c[8]], dt_rep, w[:, c[10]:c[13]],
                            w[:, c[8]:c[9]], w[:, c[0]:c[3]]], axis=1).astype(BF16)


def _trunk(x, norm1_w, w_in, hg_lb_logits, hg_norm_w, ssd_conv_w, ssd_conv_b, ssd_dt_bias,
           ssd_a_log, ssd_d, ssd_norm_w, w_branch_sb, w_branch_hg, w_branch_ssd, w_out,
           norm2_w, peer_w_q, peer_sub_keys, peer_u, peer_v, final_norm_w, tiles):
    batch, seq, d = x.shape
    depth = w_in.shape[0]
    gamma = jax.nn.softmax(hg_lb_logits.astype(F32), axis=0)
    lower_bounds = jnp.cumsum(gamma, axis=0) - gamma[0]
    xt = x.reshape(batch * seq, d)
    for l in range(depth):
        p = dict(
            norm1_w=norm1_w[l], w_in=_prep_w_in(w_in[l]), hg_norm_w=hg_norm_w[l],
            ssd_conv_w=ssd_conv_w[l], ssd_conv_b=ssd_conv_b[l], ssd_dt_bias=ssd_dt_bias[l],
            ssd_a_log=ssd_a_log[l], ssd_d=ssd_d[l], ssd_norm_w=ssd_norm_w[l],
            w_branch_sb=w_branch_sb[l].astype(BF16), w_branch_hg=w_branch_hg[l].astype(BF16),
            w_branch_ssd=w_branch_ssd[l].astype(BF16), w_out=w_out[l].astype(BF16),
            norm2_w=norm2_w[l], peer_w_q=peer_w_q[l].astype(BF16),
            peer_sub_keys=peer_sub_keys[l].reshape(2 * PEER_HEADS, PEER_NKEYS, -1).astype(BF16),
            peer_u=peer_u[l].astype(BF16),
            peer_vt=peer_v[l].reshape(-1, tiles["pe_eb"], d).transpose(0, 2, 1).astype(BF16))
        xt = _layer(xt, batch, seq, lower_bounds[l], p, final_norm_w, l == depth - 1, tiles)
    return xt.reshape(batch, seq, d)


def kernel(x, norm1_w, w_in, hg_lb_logits, hg_norm_w, ssd_conv_w, ssd_conv_b, ssd_dt_bias, ssd_a_log, ssd_d, ssd_norm_w, w_branch_sb, w_branch_hg, w_branch_ssd, w_out, norm2_w, peer_w_q, peer_sub_keys, peer_u, peer_v, final_norm_w):
    return _trunk(x, norm1_w, w_in, hg_lb_logits, hg_norm_w, ssd_conv_w, ssd_conv_b, ssd_dt_bias,
                  ssd_a_log, ssd_d, ssd_norm_w, w_branch_sb, w_branch_hg, w_branch_ssd, w_out,
                  norm2_w, peer_w_q, peer_sub_keys, peer_u, peer_v, final_norm_w, TILES)
```

```python
import functools

import numpy as np
import jax
import jax.numpy as jnp
from jax import lax
from jax.experimental import pallas as pl
from jax.experimental.pallas import tpu as pltpu

F32 = jnp.float32
BF16 = jnp.bfloat16

D_MODEL = 1024
SB_HEAD_DIM = 64
SB_WIDTH = 512
HG_HEADS = 4
HG_DK = 128
HG_CHUNK = 64
HG_F_FLOOR = 1e-30
SSD_HEADS = 8
SSD_HEAD_DIM = 64
SSD_WIDTH = 512
SSD_STATE = 64
SSD_CONV = 4
SSD_CHUNK = 128
SSD_CONV_DIM = 768
PEER_HEADS = 8
PEER_NKEYS = 128
PEER_TOPK = 16
EPS = 1e-6
NEG_BIG = -1e30

LANES = 128
SUBLANES = 8
VMEM_LIMIT = 56 * 1024 * 1024

COL_HG = 0
COL_Z = 2048
COL_DT = 2560
COL_GATE = 3072
COL_XBC = 6144
COL_SB = 6912
IN_COLS_PADDED = 8448


def _cparams(sem):
    return pltpu.CompilerParams(dimension_semantics=sem, vmem_limit_bytes=VMEM_LIMIT)


def _dot(a, b):
    return jnp.dot(a, b, preferred_element_type=F32)


def _dot_nt(a, b):
    return lax.dot_general(a, b, (((1,), (1,)), ((), ())), preferred_element_type=F32)


def _dot_tn(a, b):
    return lax.dot_general(a, b, (((0,), (0,)), ((), ())), preferred_element_type=F32)


def _sigmoid_pair(x):
    e = jnp.exp(-jnp.abs(x))
    inv = 1.0 / (1.0 + e)
    small = e * inv
    pos = x >= 0
    return jnp.where(pos, inv, small), jnp.where(pos, small, inv)


def _silu(x):
    return x * _sigmoid_pair(x)[0]


def _split3(x, axis):
    hi = x.astype(BF16)
    r1 = x - hi.astype(F32)
    mid = r1.astype(BF16)
    lo = (r1 - mid.astype(F32)).astype(BF16)
    return jnp.concatenate([hi, mid, lo], axis=axis)


def _inproj_kernel(x_ref, nw_ref, w_ref, o_ref, h_ref):
    @pl.when(pl.program_id(1) == 0)
    def _():
        x = x_ref[...]
        ms = jnp.mean(x * x, axis=-1, keepdims=True)
        h_ref[...] = (x * lax.rsqrt(ms + EPS) * nw_ref[...]).astype(BF16)

    o_ref[...] = _dot(h_ref[...], w_ref[...])


def _inproj(x, norm_w, w, tm, tn):
    T, D = x.shape
    N = w.shape[1]
    return pl.pallas_call(
        _inproj_kernel,
        grid=(T // tm, N // tn),
        in_specs=[
            pl.BlockSpec((tm, D), lambda i, j: (i, 0)),
            pl.BlockSpec((1, D), lambda i, j: (0, 0)),
            pl.BlockSpec((D, tn), lambda i, j: (0, j)),
        ],
        out_specs=pl.BlockSpec((tm, tn), lambda i, j: (i, j)),
        out_shape=jax.ShapeDtypeStruct((T, N), F32),
        scratch_shapes=[pltpu.VMEM((tm, D), BF16)],
        compiler_params=_cparams(("parallel", "arbitrary")),
    )(x, norm_w.reshape(1, D), w)


SB_EXIT = -110.0


def _sb_kernel(q_ref, k_ref, v_ref, mcat_ref, o_ref, c_ref, acc_ref, *, blk, nsub):
    qi = pl.program_id(2)
    lane = lax.broadcasted_iota(jnp.int32, (1, LANES), 1)
    first_head = lane < SB_HEAD_DIM
    mcat = mcat_ref[...]
    qparts = []
    for s in range(nsub):
        q = q_ref[s * blk:(s + 1) * blk, :] * (SB_HEAD_DIM ** -0.5)
        qparts += [jnp.where(first_head, q, 0.0), jnp.where(first_head, 0.0, q)]
    qs = jnp.concatenate(qparts, axis=0).astype(BF16)
    c_ref[...] = jnp.zeros_like(c_ref)
    acc_ref[...] = jnp.zeros_like(acc_ref)

    def visit(kb, s0, diag):
        r0 = s0 * 2 * blk
        ks = pl.multiple_of(kb * blk, blk)
        kblk = k_ref[pl.ds(ks, blk), :].astype(BF16)
        vblk = v_ref[pl.ds(ks, blk), :].astype(BF16)
        z = _dot_nt(qs[r0:], kblk)
        t = jnp.log1p(jnp.exp(-jnp.abs(z)))
        log_beta = jnp.minimum(z, 0.0) - t
        log_rest = jnp.minimum(-z, 0.0) - t
        if diag:
            row = lax.broadcasted_iota(jnp.int32, z.shape, 0)
            col = lax.broadcasted_iota(jnp.int32, z.shape, 1)
            before = (row >= 2 * blk) | (col < (row & (blk - 1)))
            log_rest = jnp.where(before, log_rest, 0.0)
        hi = log_rest.astype(BF16)
        lo = (log_rest - hi.astype(F32)).astype(BF16)
        cs = _dot(jnp.concatenate([hi, lo], axis=1), mcat)
        carry = c_ref[r0:, :]
        a = jnp.exp(log_beta + cs[:, :blk] + carry)
        if diag:
            a = jnp.where(before, a, 0.0)
        carry = carry + cs[:, blk:]
        c_ref[r0:, :] = carry
        av = _dot(a.astype(BF16), vblk)
        for s in range(s0, nsub):
            i = (s - s0) * 2 * blk
            acc_ref[s * blk:(s + 1) * blk, :] += jnp.where(first_head, av[i:i + blk], av[i + blk:i + 2 * blk])
        return carry

    for j in range(nsub - 1, -1, -1):
        carry = visit(qi * nsub + j, j, True)

    def body(state):
        kb, _ = state
        return kb - 1, jnp.max(visit(kb, 0, False))

    lax.while_loop(lambda st: (st[0] >= 0) & (st[1] > SB_EXIT), body,
                   (qi * nsub - 1, jnp.max(carry)))
    o_ref[...] = acc_ref[...]


def _sb_consts(blk):
    r = np.arange(2 * blk)[:, None] % blk
    c = np.arange(2 * blk)[None, :]
    return jnp.asarray(((c >= blk) | (r > c)).astype(np.float32), dtype=BF16)


def _sb_attention(proj, batch, seq, blk, nsub):
    T = proj.shape[0]
    tq = blk * nsub
    nq = seq // tq
    cb = COL_SB // LANES
    npair = SB_WIDTH // LANES
    return pl.pallas_call(
        functools.partial(_sb_kernel, blk=blk, nsub=nsub),
        grid=(batch, npair, nq),
        in_specs=[
            pl.BlockSpec((tq, LANES), lambda b, p, i: (b * nq + i, cb + p)),
            pl.BlockSpec((seq, LANES), lambda b, p, i: (b, cb + npair + p)),
            pl.BlockSpec((seq, LANES), lambda b, p, i: (b, cb + 2 * npair + p)),
            pl.BlockSpec((2 * blk, 2 * blk), lambda b, p, i: (0, 0)),
        ],
        out_specs=pl.BlockSpec((tq, LANES), lambda b, p, i: (b * nq + i, p)),
        out_shape=jax.ShapeDtypeStruct((T, SB_WIDTH), F32),
        scratch_shapes=[pltpu.VMEM((nsub * 2 * blk, blk), F32), pltpu.VMEM((tq, LANES), F32)],
        compiler_params=_cparams(("parallel", "parallel", "arbitrary")),
    )(proj, proj, proj, _sb_consts(blk))


HG_LEVELS = (32, 16, 8, 4, 2, 1)


def _hg_consts():
    C = HG_CHUNK
    i = np.arange(C)
    sels = [i[None, :] <= i[:, None]]
    masks = []
    for m in HG_LEVELS:
        g = (i // (2 * m)) * (2 * m) + m - 1
        sels.append(i[None, :] <= g[:, None])
        same = (i[:, None] // (2 * m)) == (i[None, :] // (2 * m))
        upper_t = (i[:, None] % (2 * m)) >= m
        lower_s = (i[None, :] % (2 * m)) < m
        masks.append(same & upper_t & lower_s)
    masks.append(np.eye(C, dtype=bool))
    big = np.concatenate(sels, 0).astype(np.float32)
    big3 = np.concatenate([big, big, big], 1)
    return jnp.asarray(big3, dtype=BF16), jnp.asarray(np.stack(masks).astype(np.float32))


def _hg_kernel(f_ref, q_ref, i_ref, g_ref, lb_ref, nw_ref, sel_ref, msk_ref, o_ref, st_ref, *, nchunk):
    C = HG_CHUNK

    @pl.when(pl.program_id(1) == 0)
    def _():
        st_ref[...] = jnp.zeros_like(st_ref)

    sel = sel_ref[...]
    nw = nw_ref[...]
    for c in range(nchunk):
        rows = slice(c * C, (c + 1) * C)
        for h in range(HG_HEADS):
            cols = slice(h * HG_DK, (h + 1) * HG_DK)
            lb = lb_ref[:, cols]
            sp, sn = _sigmoid_pair(f_ref[rows, cols])
            f = lb + (1.0 - lb) * sp
            log_f = jnp.log(jnp.maximum(f, HG_F_FLOOR))
            key = (1.0 - lb) * sn
            q = _silu(q_ref[rows, cols]) * (HG_DK ** -0.5)
            val = i_ref[rows, cols].astype(BF16)
            br = _dot(sel, _split3(log_f, 0))
            b = br[:C]
            s = msk_ref[len(HG_LEVELS)] * _dot_nt(q.astype(BF16), key.astype(BF16))
            for lvl in range(len(HG_LEVELS)):
                r = br[(lvl + 1) * C:(lvl + 2) * C]
                qm = (q * jnp.exp(jnp.minimum(b - r, 0.0))).astype(BF16)
                km = (key * jnp.exp(jnp.minimum(r - b, 0.0))).astype(BF16)
                s = s + msk_ref[lvl] * _dot_nt(qm, km)
            st = st_ref[h]
            o = _dot(s.astype(BF16), val)
            o = o + _dot_nt((q * jnp.exp(b)).astype(BF16), st.astype(BF16))
            b_last = b[C - 1:C, :]
            kt = (key * jnp.exp(b_last - b)).astype(BF16)
            st_ref[h] = st * jnp.exp(b_last) + _dot_tn(val, kt)
            ms = jnp.mean(o * o, axis=-1, keepdims=True)
            o = o * lax.rsqrt(ms + EPS) * nw
            o_ref[rows, cols] = o * _silu(g_ref[rows, cols])


def _hgrn2(proj, lb, norm_w, batch, seq, ct):
    T = proj.shape[0]
    W = HG_HEADS * HG_DK
    nc = seq // ct
    sel, msk = _hg_consts()
    cb = COL_HG // W
    spec = lambda k: pl.BlockSpec((ct, W), lambda b, c: (b * nc + c, cb + k))
    return pl.pallas_call(
        functools.partial(_hg_kernel, nchunk=ct // HG_CHUNK),
        grid=(batch, nc),
        in_specs=[spec(0), spec(1), spec(2), spec(3),
                  pl.BlockSpec((1, W), lambda b, c: (0, 0)),
                  pl.BlockSpec((1, HG_DK), lambda b, c: (0, 0)),
                  pl.BlockSpec(sel.shape, lambda b, c: (0, 0)),
                  pl.BlockSpec(msk.shape, lambda b, c: (0, 0, 0))],
        out_specs=pl.BlockSpec((ct, W), lambda b, c: (b * nc + c, 0)),
        out_shape=jax.ShapeDtypeStruct((T, W), F32),
        scratch_shapes=[pltpu.VMEM((HG_HEADS, HG_DK, HG_DK), F32)],
        compiler_params=_cparams(("parallel", "arbitrary")),
    )(proj, proj, proj, proj, lb.reshape(1, W), norm_w.reshape(1, HG_DK), sel, msk)


SSD_PAD = 8


def _ssd_kernel(z_ref, xbc_ref, dt_ref, cw_ref, cb_ref, dtb_ref, a_ref, d_ref, nw_ref, tril3_ref,
                o_ref, pad_ref, st_ref):
    L = SSD_CHUNK
    W = SSD_WIDTH
    GN = 2 * SSD_STATE

    @pl.when(pl.program_id(1) == 0)
    def _():
        pad_ref[0:SSD_PAD, :] = jnp.zeros((SSD_PAD, SSD_CONV_DIM), F32)
        st_ref[...] = jnp.zeros_like(st_ref)

    pad_ref[SSD_PAD:SSD_PAD + L, :] = xbc_ref[...]
    conv = cb_ref[...]
    for k in range(SSD_CONV):
        off = SSD_PAD - (SSD_CONV - 1) + k
        conv = conv + cw_ref[k:k + 1, :] * pad_ref[off:off + L, :]
    pad_ref[0:SSD_PAD, :] = pad_ref[L:L + SSD_PAD, :]
    xa = _silu(conv)
    xs = xa[:, :W]
    bm = xa[:, W:W + GN]
    cm = xa[:, W + GN:W + 2 * GN]
    dtv = dt_ref[...] + dtb_ref[...]
    dt = jnp.maximum(dtv, 0.0) + jnp.log1p(jnp.exp(-jnp.abs(dtv)))
    a = a_ref[...] * dt
    tril3 = tril3_ref[...]
    a_cum = _dot(tril3, _split3(a, 0))
    a_last = a_cum[L - 1:L, :]
    xdt = xs * dt

    lane = lax.broadcasted_iota(jnp.int32, (1, LANES), 1)
    half = [lane < SSD_STATE, lane >= SSD_STATE]
    row = lax.broadcasted_iota(jnp.int32, (L, L), 0)
    col = lax.broadcasted_iota(jnp.int32, (L, L), 1)
    strict = col < row
    causal = col <= row
    bb = bm.astype(BF16)
    cbm = cm.astype(BF16)
    cb_g = [_dot_nt(jnp.where(half[g], cm, 0.0).astype(BF16), bb) for g in range(2)]

    y_parts = []
    for pair in range(SSD_HEADS // 2):
        g = pair // 2
        xp = xdt[:, pair * LANES:(pair + 1) * LANES]
        acc = jnp.zeros((L, LANES), F32)
        for hh in range(2):
            h = 2 * pair + hh
            a_col = a[:, h * SSD_HEAD_DIM:h * SSD_HEAD_DIM + 1]
            diff = _dot(tril3, _split3(jnp.where(strict, a_col, 0.0), 0))
            seg = jnp.exp(jnp.where(causal, diff, NEG_BIG))
            sc = (cb_g[g] * seg).astype(BF16)
            acc = acc + _dot(sc, jnp.where(half[hh], xp, 0.0).astype(BF16))
        y_parts.append(acc)
    y = jnp.concatenate(y_parts, axis=1)

    st = st_ref[...]
    y = y + jnp.exp(a_cum) * _dot(cbm, st.astype(BF16))
    srow = lax.broadcasted_iota(jnp.int32, (GN, W), 0) // SSD_STATE
    scol = lax.broadcasted_iota(jnp.int32, (GN, W), 1) // (W // 2)
    upd = _dot_tn(bb, (jnp.exp(a_last - a_cum) * xdt).astype(BF16))
    st_ref[...] = st * jnp.exp(a_last) + jnp.where(srow == scol, upd, 0.0)

    y = y + d_ref[...] * xs
    y = y * _silu(z_ref[...])
    nw = nw_ref[...]
    outs = []
    for g in range(2):
        yg = y[:, g * (W // 2):(g + 1) * (W // 2)]
        ms = jnp.mean(yg * yg, axis=-1, keepdims=True)
        outs.append(yg * lax.rsqrt(ms + EPS) * nw[:, g * (W // 2):(g + 1) * (W // 2)])
    o_ref[...] = jnp.concatenate(outs, axis=1)


def _ssd(proj, conv_w, conv_b, dt_bias, a_log, d_skip, norm_w, batch, seq):
    T = proj.shape[0]
    L = SSD_CHUNK
    W = SSD_WIDTH
    nc = seq // L
    rep = lambda v: jnp.repeat(v.astype(F32), SSD_HEAD_DIM).reshape(1, W)
    tril = np.tril(np.ones((L, L), np.float32))
    tril3 = jnp.asarray(np.concatenate([tril, tril, tril], 1), dtype=BF16)
    vec = lambda n: pl.BlockSpec((1, n), lambda b, c: (0, 0))
    return pl.pallas_call(
        _ssd_kernel,
        grid=(batch, nc),
        in_specs=[
            pl.BlockSpec((L, W), lambda b, c: (b * nc + c, COL_Z // W)),
            pl.BlockSpec((L, SSD_CONV_DIM), lambda b, c: (b * nc + c, COL_XBC // SSD_CONV_DIM)),
            pl.BlockSpec((L, W), lambda b, c: (b * nc + c, COL_DT // W)),
            pl.BlockSpec((SSD_CONV, SSD_CONV_DIM), lambda b, c: (0, 0)),
            vec(SSD_CONV_DIM), vec(W), vec(W), vec(W), vec(W),
            pl.BlockSpec((L, 3 * L), lambda b, c: (0, 0)),
        ],
        out_specs=pl.BlockSpec((L, W), lambda b, c: (b * nc + c, 0)),
        out_shape=jax.ShapeDtypeStruct((T, W), F32),
        scratch_shapes=[pltpu.VMEM((L + SSD_PAD, SSD_CONV_DIM), F32),
                        pltpu.VMEM((2 * SSD_STATE, W), F32)],
        compiler_params=_cparams(("parallel", "arbitrary")),
    )(proj, proj, proj, conv_w, conv_b.reshape(1, SSD_CONV_DIM), rep(dt_bias),
      rep(-jnp.exp(a_log.astype(F32))), rep(d_skip), norm_w.reshape(1, W), tril3)


def _merge_kernel(ysb_ref, yhg_ref, yssd_ref, g0_ref, g1_ref, g2_ref, x_ref,
                  w0_ref, w1_ref, w2_ref, wo_ref, nw_ref, xo_ref, h_ref):
    m = _sigmoid_pair(g0_ref[...])[0] * _dot(ysb_ref[...].astype(BF16), w0_ref[...])
    m = m + _sigmoid_pair(g1_ref[...])[0] * _dot(yhg_ref[...].astype(BF16), w1_ref[...])
    m = m + _sigmoid_pair(g2_ref[...])[0] * _dot(yssd_ref[...].astype(BF16), w2_ref[...])
    xn = x_ref[...] + _dot(m.astype(BF16), wo_ref[...])
    xo_ref[...] = xn
    ms = jnp.mean(xn * xn, axis=-1, keepdims=True)
    h_ref[...] = (xn * lax.rsqrt(ms + EPS) * nw_ref[...]).astype(BF16)


def _merge(y_sb, y_hg, y_ssd, proj, x, w_sb, w_hg, w_ssd, w_out, norm_w, tm):
    T, D = x.shape
    Wb = y_sb.shape[1]
    yspec = pl.BlockSpec((tm, Wb), lambda i: (i, 0))
    gspec = lambda k: pl.BlockSpec((tm, D), lambda i: (i, COL_GATE // D + k))
    wspec = pl.BlockSpec((Wb, D), lambda i: (0, 0))
    xspec = pl.BlockSpec((tm, D), lambda i: (i, 0))
    return pl.pallas_call(
        _merge_kernel,
        grid=(T // tm,),
        in_specs=[yspec, yspec, yspec, gspec(0), gspec(1), gspec(2), xspec,
                  wspec, wspec, wspec,
                  pl.BlockSpec((D, D), lambda i: (0, 0)),
                  pl.BlockSpec((1, D), lambda i: (0, 0))],
        out_specs=[xspec, xspec],
        out_shape=[jax.ShapeDtypeStruct((T, D), F32), jax.ShapeDtypeStruct((T, D), BF16)],
        compiler_params=_cparams(("parallel",)),
    )(y_sb, y_hg, y_ssd, proj, proj, proj, x, w_sb, w_hg, w_ssd, w_out, norm_w.reshape(1, D))


def _peerq_kernel(h_ref, wq_ref, sk_ref, s1_ref, s2_ref, ht_ref):
    h = h_ref[...]
    q = _dot(h, wq_ref[...]).astype(BF16)
    for hd in range(PEER_HEADS):
        for p, s_ref in enumerate((s1_ref, s2_ref)):
            j = hd * 2 + p
            s_ref[hd] = _dot_nt(sk_ref[j], q[:, j * PEER_NKEYS:(j + 1) * PEER_NKEYS])
    ht_ref[...] = h.astype(F32).T.astype(BF16)


def _peer_query(h2, w_q, sub_keys, tm):
    T, D = h2.shape
    NK = PEER_NKEYS
    sspec = pl.BlockSpec((PEER_HEADS, NK, tm), lambda i: (0, 0, i))
    return pl.pallas_call(
        _peerq_kernel,
        grid=(T // tm,),
        in_specs=[pl.BlockSpec((tm, D), lambda i: (i, 0)),
                  pl.BlockSpec(w_q.shape, lambda i: (0, 0)),
                  pl.BlockSpec(sub_keys.shape, lambda i: (0, 0, 0))],
        out_specs=[sspec, sspec, pl.BlockSpec((None, D, tm), lambda i: (i, 0, 0))],
        out_shape=[jax.ShapeDtypeStruct((PEER_HEADS, NK, T), F32),
                   jax.ShapeDtypeStruct((PEER_HEADS, NK, T), F32),
                   jax.ShapeDtypeStruct((T // tm, D, tm), BF16)],
        compiler_params=_cparams(("parallel",)),
    )(h2, w_q, sub_keys)


def _oddeven_merge(lo, hi, r):
    step = r * 2
    if step < hi - lo:
        yield from _oddeven_merge(lo, hi, step)
        yield from _oddeven_merge(lo + r, hi, step)
        yield from [(i, i + r) for i in range(lo + r, hi - r, step)]
    else:
        yield (lo, lo + r)


def _oddeven_sort(lo, hi):
    if hi - lo >= 1:
        mid = lo + (hi - lo) // 2
        yield from _oddeven_sort(lo, mid)
        yield from _oddeven_sort(mid + 1, hi)
        yield from _oddeven_merge(lo, hi, 1)


SORT16 = tuple(_oddeven_sort(0, PEER_TOPK - 1))


def _cmp_exchange(x, i, j):
    x[i], x[j] = jnp.maximum(x[i], x[j]), jnp.minimum(x[i], x[j])


def _top16(x, n_valid):
    K = PEER_TOPK
    for i, j in SORT16:
        if j < n_valid:
            _cmp_exchange(x, i, j)
    for shift in (4, 2, 1):
        y = [pltpu.roll(v, shift, 0) for v in x]
        x = [jnp.maximum(x[i], y[K - 1 - i]) for i in range(K)]
        d = K // 2
        while d >= 1:
            for i in range(K):
                if i & d == 0:
                    _cmp_exchange(x, i, i + d)
            d //= 2
    return x


def _pack_rows(rep, sub):
    out = rep[0]
    for i in range(1, SUBLANES):
        out = jnp.where(sub == i, rep[i], out)
    return out


def _peer_topk_kernel(s1_ref, s2_ref, rank_ref, e2_ref, n_ref, r1_ref):
    for hd in range(s1_ref.shape[0]):
        _peer_topk_head(hd, s1_ref, s2_ref, rank_ref, e2_ref, n_ref, r1_ref)


def _peer_topk_head(hd, s1_ref, s2_ref, rank_ref, e2_ref, n_ref, r1_ref):
    K = PEER_TOPK
    G = PEER_NKEYS // SUBLANES
    tt = s1_ref.shape[2]
    s1 = [s1_ref[hd, g * SUBLANES:(g + 1) * SUBLANES, :] for g in range(G)]
    s2 = [s2_ref[hd, g * SUBLANES:(g + 1) * SUBLANES, :] for g in range(G)]
    v1 = _top16(list(s1), G)
    v2 = _top16(list(s2), G)
    sub = lax.broadcasted_iota(jnp.int32, (SUBLANES, tt), 0)
    v1p = [_pack_rows(v1[:SUBLANES], sub), _pack_rows(v1[SUBLANES:], sub)]
    v2p_hi = _pack_rows(v2[SUBLANES:], sub)
    cand = [v1p[0] + v2[0], v1p[1] + v2[0]] + [v1p[0] + v2[j] for j in range(1, SUBLANES)]
    cand.append(v1[0] + v2p_hi)
    ninf = jnp.full((SUBLANES, tt), -jnp.inf, F32)
    top = _top16(cand + [ninf] * (K - len(cand)), len(cand))
    tau = top[K - 1]
    zsum = jnp.ones_like(tau)
    for r in range(1, K):
        zsum = zsum + jnp.exp(top[r] - top[0])
    cnt = [jnp.zeros((SUBLANES, tt), F32), jnp.zeros((SUBLANES, tt), F32)]
    for j in range(K):
        for k in range(2):
            cnt[k] = cnt[k] + jnp.where(v1p[k] + v2[j] >= tau, 1.0, 0.0)
    n_rep = [jnp.broadcast_to(cnt[i // SUBLANES][i % SUBLANES:i % SUBLANES + 1, :], (SUBLANES, tt))
             for i in range(K)]
    n_rep = [n_rep[0]] + [jnp.where(v1[i] < v1[i - 1], n_rep[i], 0.0) for i in range(1, K)]
    inv_z = 1.0 / zsum
    for g in range(G):
        rows = slice(g * SUBLANES, (g + 1) * SUBLANES)
        n = jnp.zeros((SUBLANES, tt), F32)
        for i in range(K - 1, -1, -1):
            n = jnp.where(s1[g] == v1[i], n_rep[i], n)
        n_ref[hd, rows, :] = n
        r1_ref[hd, rows, :] = jnp.exp(s1[g] - v1[0]) * inv_z
    for g in range(0, G, 2):
        rk, e2 = [], []
        for gg in (g, g + 1):
            r = jnp.zeros((SUBLANES, tt), F32)
            for j in range(K):
                r = r + jnp.where(v2[j] > s2[gg], 1.0, 0.0)
            rk.append(r)
            e2.append(jnp.exp(s2[gg] - v2[0]))
        rows = slice(g * SUBLANES, (g + 2) * SUBLANES)
        rank_ref[hd, rows, :] = jnp.concatenate(rk, axis=0).astype(BF16)
        e2_ref[hd, rows, :] = jnp.concatenate(e2, axis=0).astype(BF16)


def _peer_topk(s1t, s2t, tt, hb):
    H, NK, T = s1t.shape
    sspec = pl.BlockSpec((hb, NK, tt), lambda i, h: (h, 0, i))
    return pl.pallas_call(
        _peer_topk_kernel,
        grid=(T // tt, H // hb),
        in_specs=[sspec, sspec],
        out_specs=[sspec, sspec, sspec, sspec],
        out_shape=[jax.ShapeDtypeStruct((H, NK, T), BF16), jax.ShapeDtypeStruct((H, NK, T), BF16),
                   jax.ShapeDtypeStruct((H, NK, T), F32), jax.ShapeDtypeStruct((H, NK, T), F32)],
        compiler_params=_cparams(("parallel", "parallel")),
    )(s1t, s2t)


def _peer_kernel(x_ref, ht_ref, rank_ref, e2_ref, n_ref, r1_ref, u0_ref, ua_ref, ub_ref, va_ref, vb_ref,
                 vl_ref, fw_ref, o_ref, act_a, act_b, p_a, p_b, acc_ref, *, eb, tt, final):
    s = pl.program_id(1)
    NK = PEER_NKEYS
    assert eb // NK == SUBLANES
    zero = jnp.zeros((), BF16)

    def build_p(blk, act_ref, p_ref, tc):
        a0 = pl.multiple_of(blk * SUBLANES, SUBLANES)
        ts = slice(tc * LANES, (tc + 1) * LANES)
        n8 = [n_ref[h, pl.ds(a0, SUBLANES), ts] for h in range(PEER_HEADS)]
        r8 = [r1_ref[h, pl.ds(a0, SUBLANES), ts] for h in range(PEER_HEADS)]
        for ai in range(SUBLANES):
            w = None
            for h in range(PEER_HEADS):
                n_row = n8[h][ai:ai + 1, :].astype(BF16)
                r_row = r8[h][ai:ai + 1, :].astype(BF16)
                term = jnp.where(rank_ref[h, :, ts] < n_row, e2_ref[h, :, ts] * r_row, zero)
                w = term if w is None else w + term
            act = act_ref[ai * NK:(ai + 1) * NK, ts]
            gelu = 0.5 * act * (1.0 + lax.erf(act * (2.0 ** -0.5)))
            p_ref[ai * NK:(ai + 1) * NK, ts] = w * gelu.astype(BF16)

    @pl.when(s == 0)
    def _():
        acc_ref[...] = jnp.zeros_like(acc_ref)
        p_b[...] = jnp.zeros_like(p_b)
        act_a[...] = _dot(u0_ref[...], ht_ref[...])

    half = tt // 2
    per_half = half // LANES

    def stage(blk, v_ref, p_prev, u_ref, act_next, act_cur, p_cur):
        for c in range(2):
            cs = slice(c * half, (c + 1) * half)
            acc_ref[:, cs] += _dot(v_ref[...], p_prev[:, cs])
            for tc in range(c * per_half, c * per_half + per_half // 2):
                build_p(blk, act_cur, p_cur, tc)
            act_next[:, cs] = _dot(u_ref[...], ht_ref[:, cs])
            for tc in range(c * per_half + per_half // 2, (c + 1) * per_half):
                build_p(blk, act_cur, p_cur, tc)

    stage(2 * s, va_ref, p_b, ua_ref, act_b, act_a, p_a)
    stage(2 * s + 1, vb_ref, p_a, ub_ref, act_a, act_b, p_b)

    @pl.when(s == pl.num_programs(1) - 1)
    def _():
        y = x_ref[...] + (acc_ref[...] + _dot(vl_ref[...], p_b[...])).T
        if final:
            ms = jnp.mean(y * y, axis=-1, keepdims=True)
            y = y * lax.rsqrt(ms + EPS) * fw_ref[...]
        o_ref[...] = y


def _peer(x, h2t, rank2, e2, n1, r1, u, vt, final_w, tt, eb, final):
    T, D = x.shape
    H, NK = PEER_HEADS, PEER_NKEYS
    ne = vt.shape[0]
    assert h2t.shape == (T // tt, D, tt) and vt.shape == (ne, D, eb) and u.shape == (ne * eb, D)
    sspec = pl.BlockSpec((H, NK, tt), lambda i, s: (0, 0, i))
    once = pl.Buffered(1)
    return pl.pallas_call(
        functools.partial(_peer_kernel, eb=eb, tt=tt, final=final),
        grid=(T // tt, ne // 2),
        in_specs=[pl.BlockSpec((tt, D), lambda i, s: (i, 0)),
                  pl.BlockSpec((None, D, tt), lambda i, s: (i, 0, 0)),
                  sspec, sspec, sspec, sspec,
                  pl.BlockSpec((eb, D), lambda i, s: (0, 0), pipeline_mode=once),
                  pl.BlockSpec((eb, D), lambda i, s: (2 * s + 1, 0)),
                  pl.BlockSpec((eb, D), lambda i, s: (jnp.minimum(2 * s + 2, ne - 1), 0)),
                  pl.BlockSpec((None, D, eb), lambda i, s: (jnp.maximum(2 * s - 1, 0), 0, 0)),
                  pl.BlockSpec((None, D, eb), lambda i, s: (2 * s, 0, 0)),
                  pl.BlockSpec((None, D, eb), lambda i, s: (ne - 1, 0, 0), pipeline_mode=once),
                  pl.BlockSpec((1, D), lambda i, s: (0, 0))],
        out_specs=pl.BlockSpec((tt, D), lambda i, s: (i, 0)),
        out_shape=jax.ShapeDtypeStruct((T, D), F32),
        scratch_shapes=[pltpu.VMEM((eb, tt), F32), pltpu.VMEM((eb, tt), F32),
                        pltpu.VMEM((eb, tt), BF16), pltpu.VMEM((eb, tt), BF16),
                        pltpu.VMEM((D, tt), F32)],
        compiler_params=_cparams(("parallel", "arbitrary")),
    )(x, h2t, rank2, e2, n1, r1, u, u, u, vt, vt, vt, final_w.reshape(1, D))


def _layer(x, batch, seq, lb, p, final_w, final, tiles):
    proj = _inproj(x, p["norm1_w"], p["w_in"], tiles["in_tm"], tiles["in_tn"])
    y_sb = _sb_attention(proj, batch, seq, tiles["sb_blk"], tiles["sb_nsub"])
    y_hg = _hgrn2(proj, lb, p["hg_norm_w"], batch, seq, tiles["hg_ct"])
    y_ssd = _ssd(proj, p["ssd_conv_w"], p["ssd_conv_b"], p["ssd_dt_bias"], p["ssd_a_log"],
                 p["ssd_d"], p["ssd_norm_w"], batch, seq)
    x, h2 = _merge(y_sb, y_hg, y_ssd, proj, x, p["w_branch_sb"], p["w_branch_hg"],
                   p["w_branch_ssd"], p["w_out"], p["norm2_w"], tiles["mg_tm"])
    s1t, s2t, h2t = _peer_query(h2, p["peer_w_q"], p["peer_sub_keys"], tiles["pq_tm"])
    rank2, e2, n1, r1 = _peer_topk(s1t, s2t, tiles["tk_tt"], tiles["tk_hb"])
    return _peer(x, h2t, rank2, e2, n1, r1, p["peer_u"], p["peer_vt"], final_w,
                 tiles["pe_tt"], tiles["pe_eb"], final)


TILES = dict(in_tm=2048, in_tn=768, sb_blk=128, sb_nsub=2, hg_ct=256, mg_tm=256, pq_tm=512,
             tk_tt=256, tk_hb=2, pe_tt=512, pe_eb=1024)


def _prep_w_in(w):
    c = np.cumsum([0, 512, 512, 512, 512, 512, 512, 512, 512, 768, 8, 1024, 1024, 1024])
    dt_rep = jnp.repeat(w[:, c[9]:c[10]], SSD_HEAD_DIM, axis=1)
    return jnp.concatenate([w[:, c[3]:c[7]], w[:, c[7]:c[8]], dt_rep, w[:, c[10]:c[13]],
                            w[:, c[8]:c[9]], w[:, c[0]:c[3]]], axis=1).astype(BF16)


def _trunk(x, norm1_w, w_in, hg_lb_logits, hg_norm_w, ssd_conv_w, ssd_conv_b, ssd_dt_bias,
           ssd_a_log, ssd_d, ssd_norm_w, w_branch_sb, w_branch_hg, w_branch_ssd, w_out,
           norm2_w, peer_w_q, peer_sub_keys, peer_u, peer_v, final_norm_w, tiles):
    batch, seq, d = x.shape
    depth = w_in.shape[0]
    gamma = jax.nn.softmax(hg_lb_logits.astype(F32), axis=0)
    lower_bounds = jnp.cumsum(gamma, axis=0) - gamma[0]
    xt = x.reshape(batch * seq, d)
    for l in range(depth):
        p = dict(
            norm1_w=norm1_w[l], w_in=_prep_w_in(w_in[l]), hg_norm_w=hg_norm_w[l],
            ssd_conv_w=ssd_conv_w[l], ssd_conv_b=ssd_conv_b[l], ssd_dt_bias=ssd_dt_bias[l],
            ssd_a_log=ssd_a_log[l], ssd_d=ssd_d[l], ssd_norm_w=ssd_norm_w[l],
            w_branch_sb=w_branch_sb[l].astype(BF16), w_branch_hg=w_branch_hg[l].astype(BF16),
            w_branch_ssd=w_branch_ssd[l].astype(BF16), w_out=w_out[l].astype(BF16),
            norm2_w=norm2_w[l], peer_w_q=peer_w_q[l].astype(BF16),
            peer_sub_keys=peer_sub_keys[l].reshape(2 * PEER_HEADS, PEER_NKEYS, -1).astype(BF16),
            peer_u=peer_u[l].astype(BF16),
            peer_vt=peer_v[l].reshape(-1, tiles["pe_eb"], d).transpose(0, 2, 1).astype(BF16))
        xt = _layer(xt, batch, seq, lower_bounds[l], p, final_norm_w, l == depth - 1, tiles)
    return xt.reshape(batch, seq, d)


def kernel(x, norm1_w, w_in, hg_lb_logits, hg_norm_w, ssd_conv_w, ssd_conv_b, ssd_dt_bias, ssd_a_log, ssd_d, ssd_norm_w, w_branch_sb, w_branch_hg, w_branch_ssd, w_out, norm2_w, peer_w_q, peer_sub_keys, peer_u, peer_v, final_norm_w):
    return _trunk(x, norm1_w, w_in, hg_lb_logits, hg_norm_w, ssd_conv_w, ssd_conv_b, ssd_dt_bias,
                  ssd_a_log, ssd_d, ssd_norm_w, w_branch_sb, w_branch_hg, w_branch_ssd, w_out,
                  norm2_w, peer_w_q, peer_sub_keys, peer_u, peer_v, final_norm_w, TILES)
```

```python
import functools

import numpy as np
import jax
import jax.numpy as jnp
from jax import lax
from jax.experimental import pallas as pl
from jax.experimental.pallas import tpu as pltpu

F32 = jnp.float32
BF16 = jnp.bfloat16

D_MODEL = 1024
SB_HEAD_DIM = 64
SB_WIDTH = 512
HG_HEADS = 4
HG_DK = 128
HG_CHUNK = 64
HG_F_FLOOR = 1e-30
SSD_HEADS = 8
SSD_HEAD_DIM = 64
SSD_WIDTH = 512
SSD_STATE = 64
SSD_CONV = 4
SSD_CHUNK = 128
SSD_CONV_DIM = 768
PEER_HEADS = 8
PEER_NKEYS = 128
PEER_TOPK = 16
EPS = 1e-6
NEG_BIG = -1e30

LANES = 128
SUBLANES = 8
VMEM_LIMIT = 56 * 1024 * 1024

COL_HG = 0
COL_Z = 2048
COL_DT = 2560
COL_GATE = 3072
COL_XBC = 6144
COL_SB = 6912
IN_COLS_PADDED = 8448


def _cparams(sem):
    return pltpu.CompilerParams(dimension_semantics=sem, vmem_limit_bytes=VMEM_LIMIT)


def _dot(a, b):
    return jnp.dot(a, b, preferred_element_type=F32)


def _dot_nt(a, b):
    return lax.dot_general(a, b, (((1,), (1,)), ((), ())), preferred_element_type=F32)


def _dot_tn(a, b):
    return lax.dot_general(a, b, (((0,), (0,)), ((), ())), preferred_element_type=F32)


def _sigmoid_pair(x):
    e = jnp.exp(-jnp.abs(x))
    inv = 1.0 / (1.0 + e)
    small = e * inv
    pos = x >= 0
    return jnp.where(pos, inv, small), jnp.where(pos, small, inv)


def _silu(x):
    return x * _sigmoid_pair(x)[0]


def _split3(x, axis):
    hi = x.astype(BF16)
    r1 = x - hi.astype(F32)
    mid = r1.astype(BF16)
    lo = (r1 - mid.astype(F32)).astype(BF16)
    return jnp.concatenate([hi, mid, lo], axis=axis)


def _inproj_kernel(x_ref, nw_ref, w_ref, o_ref, h_ref):
    @pl.when(pl.program_id(1) == 0)
    def _():
        x = x_ref[...]
        ms = jnp.mean(x * x, axis=-1, keepdims=True)
        h_ref[...] = (x * lax.rsqrt(ms + EPS) * nw_ref[...]).astype(BF16)

    o_ref[...] = _dot(h_ref[...], w_ref[...])


def _inproj(x, norm_w, w, tm, tn):
    T, D = x.shape
    N = w.shape[1]
    return pl.pallas_call(
        _inproj_kernel,
        grid=(T // tm, N // tn),
        in_specs=[
            pl.BlockSpec((tm, D), lambda i, j: (i, 0)),
            pl.BlockSpec((1, D), lambda i, j: (0, 0)),
            pl.BlockSpec((D, tn), lambda i, j: (0, j)),
        ],
        out_specs=pl.BlockSpec((tm, tn), lambda i, j: (i, j)),
        out_shape=jax.ShapeDtypeStruct((T, N), F32),
        scratch_shapes=[pltpu.VMEM((tm, D), BF16)],
        compiler_params=_cparams(("parallel", "arbitrary")),
    )(x, norm_w.reshape(1, D), w)


SB_EXIT = -110.0


def _sb_kernel(q_ref, k_ref, v_ref, mcat_ref, o_ref, c_ref, acc_ref, *, blk, nsub):
    nseq = q_ref.shape[0]
    qi = pl.program_id(1)
    lane = lax.broadcasted_iota(jnp.int32, (1, LANES), 1)
    first_head = lane < SB_HEAD_DIM
    mcat = mcat_ref[...]
    qs = []
    for g in range(nseq):
        qparts = []
        for s in range(nsub):
            q = q_ref[g, s * blk:(s + 1) * blk, :] * (SB_HEAD_DIM ** -0.5)
            qparts += [jnp.where(first_head, q, 0.0), jnp.where(first_head, 0.0, q)]
        qs.append(jnp.concatenate(qparts, axis=0).astype(BF16))
    c_ref[...] = jnp.zeros_like(c_ref)
    acc_ref[...] = jnp.zeros_like(acc_ref)

    def visit(kb, s0, diag):
        r0 = s0 * 2 * blk
        n = (nsub - s0) * 2 * blk
        ks = pl.multiple_of(kb * blk, blk)
        z = jnp.concatenate([_dot_nt(qs[g][r0:], k_ref[g, pl.ds(ks, blk), :].astype(BF16))
                             for g in range(nseq)], axis=0)
        t = jnp.log1p(jnp.exp(-jnp.abs(z)))
        log_beta = jnp.minimum(z, 0.0) - t
        log_rest = jnp.minimum(-z, 0.0) - t
        if diag:
            row = lax.rem(lax.broadcasted_iota(jnp.int32, z.shape, 0), n)
            col = lax.broadcasted_iota(jnp.int32, z.shape, 1)
            before = (row >= 2 * blk) | (col < (row & (blk - 1)))
            log_rest = jnp.where(before, log_rest, 0.0)
        hi = log_rest.astype(BF16)
        lo = (log_rest - hi.astype(F32)).astype(BF16)
        cs = _dot(jnp.concatenate([hi, lo], axis=1), mcat)
        carry = jnp.concatenate([c_ref[g, r0:, :] for g in range(nseq)], axis=0)
        a = jnp.exp(log_beta + cs[:, :blk] + carry)
        if diag:
            a = jnp.where(before, a, 0.0)
        carry = carry + cs[:, blk:]
        a = a.astype(BF16)
        for g in range(nseq):
            c_ref[g, r0:, :] = carry[g * n:(g + 1) * n]
            av = _dot(a[g * n:(g + 1) * n], v_ref[g, pl.ds(ks, blk), :].astype(BF16))
            for s in range(s0, nsub):
                i = (s - s0) * 2 * blk
                acc_ref[g, s * blk:(s + 1) * blk, :] += jnp.where(first_head, av[i:i + blk], av[i + blk:i + 2 * blk])
        return carry

    for j in range(nsub - 1, -1, -1):
        carry = visit(qi * nsub + j, j, True)

    def body(state):
        kb, _ = state
        return kb - 1, jnp.max(visit(kb, 0, False))

    lax.while_loop(lambda st: (st[0] >= 0) & (st[1] > SB_EXIT), body,
                   (qi * nsub - 1, jnp.max(carry)))
    o_ref[...] = acc_ref[...]


def _sb_consts(blk):
    r = np.arange(2 * blk)[:, None] % blk
    c = np.arange(2 * blk)[None, :]
    return jnp.asarray(((c >= blk) | (r > c)).astype(np.float32), dtype=BF16)


def _sb_attention(proj, batch, seq, blk, nsub):
    T, ncol = proj.shape
    tq = blk * nsub
    cb = COL_SB // LANES
    npair = SB_WIDTH // LANES
    proj3 = proj.reshape(batch, seq, ncol)
    out = pl.pallas_call(
        functools.partial(_sb_kernel, blk=blk, nsub=nsub),
        grid=(npair, seq // tq),
        in_specs=[
            pl.BlockSpec((batch, tq, LANES), lambda p, i: (0, i, cb + p)),
            pl.BlockSpec((batch, seq, LANES), lambda p, i: (0, 0, cb + npair + p)),
            pl.BlockSpec((batch, seq, LANES), lambda p, i: (0, 0, cb + 2 * npair + p)),
            pl.BlockSpec((2 * blk, 2 * blk), lambda p, i: (0, 0)),
        ],
        out_specs=pl.BlockSpec((batch, tq, LANES), lambda p, i: (0, i, p)),
        out_shape=jax.ShapeDtypeStruct((batch, seq, SB_WIDTH), F32),
        scratch_shapes=[pltpu.VMEM((batch, nsub * 2 * blk, blk), F32), pltpu.VMEM((batch, tq, LANES), F32)],
        compiler_params=_cparams(("parallel", "arbitrary")),
    )(proj3, proj3, proj3, _sb_consts(blk))
    return out.reshape(T, SB_WIDTH)


HG_LEVELS = (32, 16, 8, 4, 2, 1)


def _hg_consts():
    C = HG_CHUNK
    i = np.arange(C)
    sels = [i[None, :] <= i[:, None]]
    masks = []
    for m in HG_LEVELS:
        g = (i // (2 * m)) * (2 * m) + m - 1
        sels.append(i[None, :] <= g[:, None])
        same = (i[:, None] // (2 * m)) == (i[None, :] // (2 * m))
        upper_t = (i[:, None] % (2 * m)) >= m
        lower_s = (i[None, :] % (2 * m)) < m
        masks.append(same & upper_t & lower_s)
    masks.append(np.eye(C, dtype=bool))
    big = np.concatenate(sels, 0).astype(np.float32)
    big3 = np.concatenate([big, big, big], 1)
    return jnp.asarray(big3, dtype=BF16), jnp.asarray(np.stack(masks).astype(np.float32))


def _hg_kernel(f_ref, q_ref, i_ref, g_ref, lb_ref, nw_ref, sel_ref, msk_ref, o_ref, st_ref, *, nchunk):
    C = HG_CHUNK

    @pl.when(pl.program_id(1) == 0)
    def _():
        st_ref[...] = jnp.zeros_like(st_ref)

    sel = sel_ref[...]
    nw = nw_ref[...]
    lb = lb_ref[...]
    nlvl = len(HG_LEVELS)
    for c in range(nchunk):
        rows = slice(c * C, (c + 1) * C)
        sp, sn = _sigmoid_pair(f_ref[rows, :])
        f = lb + (1.0 - lb) * sp
        log_f = jnp.log(jnp.maximum(f, HG_F_FLOOR))
        key = (1.0 - lb) * sn
        q = _silu(q_ref[rows, :]) * (HG_DK ** -0.5)
        val = i_ref[rows, :].astype(BF16)
        br = _dot(sel, _split3(log_f, 0))
        b = br[:C]
        qk = []
        for lvl in range(nlvl):
            r = br[(lvl + 1) * C:(lvl + 2) * C]
            qk.append(((q * jnp.exp(jnp.minimum(b - r, 0.0))).astype(BF16),
                       (key * jnp.exp(jnp.minimum(r - b, 0.0))).astype(BF16)))
        qk.append((q.astype(BF16), key.astype(BF16)))
        q_in = (q * jnp.exp(b)).astype(BF16)
        b_last = b[C - 1:C, :]
        kt = (key * jnp.exp(b_last - b)).astype(BF16)
        decay = jnp.exp(b_last)
        outs = []
        for h in range(HG_HEADS):
            cols = slice(h * HG_DK, (h + 1) * HG_DK)
            s = msk_ref[0] * _dot_nt(qk[0][0][:, cols], qk[0][1][:, cols])
            for lvl in range(1, nlvl + 1):
                s = s + msk_ref[lvl] * _dot_nt(qk[lvl][0][:, cols], qk[lvl][1][:, cols])
            st = st_ref[h]
            o = _dot(s.astype(BF16), val[:, cols]) + _dot_nt(q_in[:, cols], st.astype(BF16))
            st_ref[h] = st * decay[:, cols] + _dot_tn(val[:, cols], kt[:, cols])
            ms = jnp.mean(o * o, axis=-1, keepdims=True)
            outs.append(o * lax.rsqrt(ms + EPS) * nw)
        o_ref[rows, :] = jnp.concatenate(outs, axis=1) * _silu(g_ref[rows, :])


def _hgrn2(proj, lb, norm_w, batch, seq, ct):
    T = proj.shape[0]
    W = HG_HEADS * HG_DK
    nc = seq // ct
    sel, msk = _hg_consts()
    cb = COL_HG // W
    spec = lambda k: pl.BlockSpec((ct, W), lambda b, c: (b * nc + c, cb + k))
    return pl.pallas_call(
        functools.partial(_hg_kernel, nchunk=ct // HG_CHUNK),
        grid=(batch, nc),
        in_specs=[spec(0), spec(1), spec(2), spec(3),
                  pl.BlockSpec((1, W), lambda b, c: (0, 0)),
                  pl.BlockSpec((1, HG_DK), lambda b, c: (0, 0)),
                  pl.BlockSpec(sel.shape, lambda b, c: (0, 0)),
                  pl.BlockSpec(msk.shape, lambda b, c: (0, 0, 0))],
        out_specs=pl.BlockSpec((ct, W), lambda b, c: (b * nc + c, 0)),
        out_shape=jax.ShapeDtypeStruct((T, W), F32),
        scratch_shapes=[pltpu.VMEM((HG_HEADS, HG_DK, HG_DK), F32)],
        compiler_params=_cparams(("parallel", "arbitrary")),
    )(proj, proj, proj, proj, lb.reshape(1, W), norm_w.reshape(1, HG_DK), sel, msk)


SSD_PAD = 8


def _ssd_kernel(z_ref, xbc_ref, dt_ref, cw_ref, cb_ref, dtb_ref, a_ref, d_ref, nw_ref, tril3_ref,
                o_ref, pad_ref, st_ref):
    L = SSD_CHUNK
    W = SSD_WIDTH
    GN = 2 * SSD_STATE

    @pl.when(pl.program_id(1) == 0)
    def _():
        pad_ref[0:SSD_PAD, :] = jnp.zeros((SSD_PAD, SSD_CONV_DIM), F32)
        st_ref[...] = jnp.zeros_like(st_ref)

    pad_ref[SSD_PAD:SSD_PAD + L, :] = xbc_ref[...]
    conv = cb_ref[...]
    for k in range(SSD_CONV):
        off = SSD_PAD - (SSD_CONV - 1) + k
        conv = conv + cw_ref[k:k + 1, :] * pad_ref[off:off + L, :]
    pad_ref[0:SSD_PAD, :] = pad_ref[L:L + SSD_PAD, :]
    xa = _silu(conv)
    xs = xa[:, :W]
    bm = xa[:, W:W + GN]
    cm = xa[:, W + GN:W + 2 * GN]
    dtv = dt_ref[...] + dtb_ref[...]
    dt = jnp.maximum(dtv, 0.0) + jnp.log1p(jnp.exp(-jnp.abs(dtv)))
    a = a_ref[...] * dt
    tril3 = tril3_ref[...]
    a_cum = _dot(tril3, _split3(a, 0))
    a_last = a_cum[L - 1:L, :]
    xdt = xs * dt

    lane = lax.broadcasted_iota(jnp.int32, (1, LANES), 1)
    half = [lane < SSD_STATE, lane >= SSD_STATE]
    row = lax.broadcasted_iota(jnp.int32, (L, L), 0)
    col = lax.broadcasted_iota(jnp.int32, (L, L), 1)
    strict = col < row
    causal = col <= row
    bb = bm.astype(BF16)
    cbm = cm.astype(BF16)
    cb_g = [_dot_nt(jnp.where(half[g], cm, 0.0).astype(BF16), bb) for g in range(2)]

    y_parts = []
    for pair in range(SSD_HEADS // 2):
        g = pair // 2
        xp = xdt[:, pair * LANES:(pair + 1) * LANES]
        acc = jnp.zeros((L, LANES), F32)
        for hh in range(2):
            h = 2 * pair + hh
            a_col = a[:, h * SSD_HEAD_DIM:h * SSD_HEAD_DIM + 1]
            diff = _dot(tril3, _split3(jnp.where(strict, a_col, 0.0), 0))
            seg = jnp.exp(jnp.where(causal, diff, NEG_BIG))
            sc = (cb_g[g] * seg).astype(BF16)
            acc = acc + _dot(sc, jnp.where(half[hh], xp, 0.0).astype(BF16))
        y_parts.append(acc)
    y = jnp.concatenate(y_parts, axis=1)

    st = st_ref[...]
    y = y + jnp.exp(a_cum) * _dot(cbm, st.astype(BF16))
    srow = lax.broadcasted_iota(jnp.int32, (GN, W), 0) // SSD_STATE
    scol = lax.broadcasted_iota(jnp.int32, (GN, W), 1) // (W // 2)
    upd = _dot_tn(bb, (jnp.exp(a_last - a_cum) * xdt).astype(BF16))
    st_ref[...] = st * jnp.exp(a_last) + jnp.where(srow == scol, upd, 0.0)

    y = y + d_ref[...] * xs
    y = y * _silu(z_ref[...])
    nw = nw_ref[...]
    outs = []
    for g in range(2):
        yg = y[:, g * (W // 2):(g + 1) * (W // 2)]
        ms = jnp.mean(yg * yg, axis=-1, keepdims=True)
        outs.append(yg * lax.rsqrt(ms + EPS) * nw[:, g * (W // 2):(g + 1) * (W // 2)])
    o_ref[...] = jnp.concatenate(outs, axis=1)


def _ssd(proj, conv_w, conv_b, dt_bias, a_log, d_skip, norm_w, batch, seq):
    T = proj.shape[0]
    L = SSD_CHUNK
    W = SSD_WIDTH
    nc = seq // L
    rep = lambda v: jnp.repeat(v.astype(F32), SSD_HEAD_DIM).reshape(1, W)
    tril = np.tril(np.ones((L, L), np.float32))
    tril3 = jnp.asarray(np.concatenate([tril, tril, tril], 1), dtype=BF16)
    vec = lambda n: pl.BlockSpec((1, n), lambda b, c: (0, 0))
    return pl.pallas_call(
        _ssd_kernel,
        grid=(batch, nc),
        in_specs=[
            pl.BlockSpec((L, W), lambda b, c: (b * nc + c, COL_Z // W)),
            pl.BlockSpec((L, SSD_CONV_DIM), lambda b, c: (b * nc + c, COL_XBC // SSD_CONV_DIM)),
            pl.BlockSpec((L, W), lambda b, c: (b * nc + c, COL_DT // W)),
            pl.BlockSpec((SSD_CONV, SSD_CONV_DIM), lambda b, c: (0, 0)),
            vec(SSD_CONV_DIM), vec(W), vec(W), vec(W), vec(W),
            pl.BlockSpec((L, 3 * L), lambda b, c: (0, 0)),
        ],
        out_specs=pl.BlockSpec((L, W), lambda b, c: (b * nc + c, 0)),
        out_shape=jax.ShapeDtypeStruct((T, W), F32),
        scratch_shapes=[pltpu.VMEM((L + SSD_PAD, SSD_CONV_DIM), F32),
                        pltpu.VMEM((2 * SSD_STATE, W), F32)],
        compiler_params=_cparams(("parallel", "arbitrary")),
    )(proj, proj, proj, conv_w, conv_b.reshape(1, SSD_CONV_DIM), rep(dt_bias),
      rep(-jnp.exp(a_log.astype(F32))), rep(d_skip), norm_w.reshape(1, W), tril3)


def _merge_kernel(ysb_ref, yhg_ref, yssd_ref, g0_ref, g1_ref, g2_ref, x_ref,
                  w0_ref, w1_ref, w2_ref, wo_ref, nw_ref, xo_ref, h_ref):
    m = _sigmoid_pair(g0_ref[...])[0] * _dot(ysb_ref[...].astype(BF16), w0_ref[...])
    m = m + _sigmoid_pair(g1_ref[...])[0] * _dot(yhg_ref[...].astype(BF16), w1_ref[...])
    m = m + _sigmoid_pair(g2_ref[...])[0] * _dot(yssd_ref[...].astype(BF16), w2_ref[...])
    xn = x_ref[...] + _dot(m.astype(BF16), wo_ref[...])
    xo_ref[...] = xn
    ms = jnp.mean(xn * xn, axis=-1, keepdims=True)
    h_ref[...] = (xn * lax.rsqrt(ms + EPS) * nw_ref[...]).astype(BF16)


def _merge(y_sb, y_hg, y_ssd, proj, x, w_sb, w_hg, w_ssd, w_out, norm_w, tm):
    T, D = x.shape
    Wb = y_sb.shape[1]
    yspec = pl.BlockSpec((tm, Wb), lambda i: (i, 0))
    gspec = lambda k: pl.BlockSpec((tm, D), lambda i: (i, COL_GATE // D + k))
    wspec = pl.BlockSpec((Wb, D), lambda i: (0, 0))
    xspec = pl.BlockSpec((tm, D), lambda i: (i, 0))
    return pl.pallas_call(
        _merge_kernel,
        grid=(T // tm,),
        in_specs=[yspec, yspec, yspec, gspec(0), gspec(1), gspec(2), xspec,
                  wspec, wspec, wspec,
                  pl.BlockSpec((D, D), lambda i: (0, 0)),
                  pl.BlockSpec((1, D), lambda i: (0, 0))],
        out_specs=[xspec, xspec],
        out_shape=[jax.ShapeDtypeStruct((T, D), F32), jax.ShapeDtypeStruct((T, D), BF16)],
        compiler_params=_cparams(("parallel",)),
    )(y_sb, y_hg, y_ssd, proj, proj, proj, x, w_sb, w_hg, w_ssd, w_out, norm_w.reshape(1, D))


def _peerq_kernel(h_ref, wq_ref, sk_ref, s1_ref, s2_ref, ht_ref):
    h = h_ref[...]
    q = _dot(h, wq_ref[...]).astype(BF16)
    for hd in range(PEER_HEADS):
        for p, s_ref in enumerate((s1_ref, s2_ref)):
            j = hd * 2 + p
            s_ref[hd] = _dot_nt(sk_ref[j], q[:, j * PEER_NKEYS:(j + 1) * PEER_NKEYS])
    ht_ref[...] = h.astype(F32).T.astype(BF16)


def _peer_query(h2, w_q, sub_keys, tm):
    T, D = h2.shape
    NK = PEER_NKEYS
    sspec = pl.BlockSpec((PEER_HEADS, NK, tm), lambda i: (0, 0, i))
    return pl.pallas_call(
        _peerq_kernel,
        grid=(T // tm,),
        in_specs=[pl.BlockSpec((tm, D), lambda i: (i, 0)),
                  pl.BlockSpec(w_q.shape, lambda i: (0, 0)),
                  pl.BlockSpec(sub_keys.shape, lambda i: (0, 0, 0))],
        out_specs=[sspec, sspec, pl.BlockSpec((None, D, tm), lambda i: (i, 0, 0))],
        out_shape=[jax.ShapeDtypeStruct((PEER_HEADS, NK, T), F32),
                   jax.ShapeDtypeStruct((PEER_HEADS, NK, T), F32),
                   jax.ShapeDtypeStruct((T // tm, D, tm), BF16)],
        compiler_params=_cparams(("parallel",)),
    )(h2, w_q, sub_keys)


def _oddeven_merge(lo, hi, r):
    step = r * 2
    if step < hi - lo:
        yield from _oddeven_merge(lo, hi, step)
        yield from _oddeven_merge(lo + r, hi, step)
        yield from [(i, i + r) for i in range(lo + r, hi - r, step)]
    else:
        yield (lo, lo + r)


def _oddeven_sort(lo, hi):
    if hi - lo >= 1:
        mid = lo + (hi - lo) // 2
        yield from _oddeven_sort(lo, mid)
        yield from _oddeven_sort(mid + 1, hi)
        yield from _oddeven_merge(lo, hi, 1)


SORT16 = tuple(_oddeven_sort(0, PEER_TOPK - 1))


def _cmp_exchange(x, i, j):
    x[i], x[j] = jnp.maximum(x[i], x[j]), jnp.minimum(x[i], x[j])


def _top16(x, n_valid):
    K = PEER_TOPK
    for i, j in SORT16:
        if j < n_valid:
            _cmp_exchange(x, i, j)
    for shift in (4, 2, 1):
        y = [pltpu.roll(v, shift, 0) for v in x]
        x = [jnp.maximum(x[i], y[K - 1 - i]) for i in range(K)]
        d = K // 2
        while d >= 1:
            for i in range(K):
                if i & d == 0:
                    _cmp_exchange(x, i, i + d)
            d //= 2
    return x


def _pack_rows(rep, sub):
    out = rep[0]
    for i in range(1, SUBLANES):
        out = jnp.where(sub == i, rep[i], out)
    return out


def _peer_topk_kernel(s1_ref, s2_ref, rank_ref, e2_ref, n_ref, r1_ref):
    for hd in range(s1_ref.shape[0]):
        _peer_topk_head(hd, s1_ref, s2_ref, rank_ref, e2_ref, n_ref, r1_ref)


def _peer_topk_head(hd, s1_ref, s2_ref, rank_ref, e2_ref, n_ref, r1_ref):
    K = PEER_TOPK
    G = PEER_NKEYS // SUBLANES
    tt = s1_ref.shape[2]
    s1 = [s1_ref[hd, g * SUBLANES:(g + 1) * SUBLANES, :] for g in range(G)]
    s2 = [s2_ref[hd, g * SUBLANES:(g + 1) * SUBLANES, :] for g in range(G)]
    v1 = _top16(list(s1), G)
    v2 = _top16(list(s2), G)
    sub = lax.broadcasted_iota(jnp.int32, (SUBLANES, tt), 0)
    v1p = [_pack_rows(v1[:SUBLANES], sub), _pack_rows(v1[SUBLANES:], sub)]
    v2p_hi = _pack_rows(v2[SUBLANES:], sub)
    cand = [v1p[0] + v2[0], v1p[1] + v2[0]] + [v1p[0] + v2[j] for j in range(1, SUBLANES)]
    cand.append(v1[0] + v2p_hi)
    ninf = jnp.full((SUBLANES, tt), -jnp.inf, F32)
    top = _top16(cand + [ninf] * (K - len(cand)), len(cand))
    tau = top[K - 1]
    zsum = jnp.ones_like(tau)
    for r in range(1, K):
        zsum = zsum + jnp.exp(top[r] - top[0])
    cnt = [jnp.zeros((SUBLANES, tt), F32), jnp.zeros((SUBLANES, tt), F32)]
    for j in range(K):
        for k in range(2):
            cnt[k] = cnt[k] + jnp.where(v1p[k] + v2[j] >= tau, 1.0, 0.0)
    n_rep = [jnp.broadcast_to(cnt[i // SUBLANES][i % SUBLANES:i % SUBLANES + 1, :], (SUBLANES, tt))
             for i in range(K)]
    n_rep = [n_rep[0]] + [jnp.where(v1[i] < v1[i - 1], n_rep[i], 0.0) for i in range(1, K)]
    inv_z = 1.0 / zsum
    for g in range(G):
        rows = slice(g * SUBLANES, (g + 1) * SUBLANES)
        n = jnp.zeros((SUBLANES, tt), F32)
        for i in range(K - 1, -1, -1):
            n = jnp.where(s1[g] == v1[i], n_rep[i], n)
        n_ref[hd, rows, :] = n
        r1_ref[hd, rows, :] = jnp.exp(s1[g] - v1[0]) * inv_z
    for g in range(0, G, 2):
        rk, e2 = [], []
        for gg in (g, g + 1):
            r = jnp.zeros((SUBLANES, tt), F32)
            for j in range(K):
                r = r + jnp.where(v2[j] > s2[gg], 1.0, 0.0)
            rk.append(r)
            e2.append(jnp.exp(s2[gg] - v2[0]))
        rows = slice(g * SUBLANES, (g + 2) * SUBLANES)
        rank_ref[hd, rows, :] = jnp.concatenate(rk, axis=0).astype(BF16)
        e2_ref[hd, rows, :] = jnp.concatenate(e2, axis=0).astype(BF16)


def _peer_topk(s1t, s2t, tt, hb):
    H, NK, T = s1t.shape
    sspec = pl.BlockSpec((hb, NK, tt), lambda i, h: (h, 0, i))
    return pl.pallas_call(
        _peer_topk_kernel,
        grid=(T // tt, H // hb),
        in_specs=[sspec, sspec],
        out_specs=[sspec, sspec, sspec, sspec],
        out_shape=[jax.ShapeDtypeStruct((H, NK, T), BF16), jax.ShapeDtypeStruct((H, NK, T), BF16),
                   jax.ShapeDtypeStruct((H, NK, T), F32), jax.ShapeDtypeStruct((H, NK, T), F32)],
        compiler_params=_cparams(("parallel", "parallel")),
    )(s1t, s2t)


def _peer_kernel(x_ref, ht_ref, rank_ref, e2_ref, n_ref, r1_ref, u0_ref, ua_ref, ub_ref, va_ref, vb_ref,
                 vl_ref, fw_ref, o_ref, act_a, act_b, p_a, p_b, acc_ref, *, eb, tt, final):
    s = pl.program_id(1)
    NK = PEER_NKEYS
    assert eb // NK == SUBLANES
    zero = jnp.zeros((), BF16)

    def build_p(blk, act_ref, p_ref, tc):
        a0 = pl.multiple_of(blk * SUBLANES, SUBLANES)
        ts = slice(tc * LANES, (tc + 1) * LANES)
        n8 = [n_ref[h, pl.ds(a0, SUBLANES), ts] for h in range(PEER_HEADS)]
        r8 = [r1_ref[h, pl.ds(a0, SUBLANES), ts] for h in range(PEER_HEADS)]
        for ai in range(SUBLANES):
            w = None
            for h in range(PEER_HEADS):
                n_row = n8[h][ai:ai + 1, :].astype(BF16)
                r_row = r8[h][ai:ai + 1, :].astype(BF16)
                term = jnp.where(rank_ref[h, :, ts] < n_row, e2_ref[h, :, ts] * r_row, zero)
                w = term if w is None else w + term
            act = act_ref[ai * NK:(ai + 1) * NK, ts]
            gelu = 0.5 * act * (1.0 + lax.erf(act * (2.0 ** -0.5)))
            p_ref[ai * NK:(ai + 1) * NK, ts] = w * gelu.astype(BF16)

    @pl.when(s == 0)
    def _():
        acc_ref[...] = jnp.zeros_like(acc_ref)
        p_b[...] = jnp.zeros_like(p_b)
        act_a[...] = _dot(u0_ref[...], ht_ref[...])

    half = tt // 2
    per_half = half // LANES

    def stage(blk, v_ref, p_prev, u_ref, act_next, act_cur, p_cur):
        for c in range(2):
            cs = slice(c * half, (c + 1) * half)
            acc_ref[:, cs] += _dot(v_ref[...], p_prev[:, cs])
            for tc in range(c * per_half, c * per_half + per_half // 2):
                build_p(blk, act_cur, p_cur, tc)
            act_next[:, cs] = _dot(u_ref[...], ht_ref[:, cs])
            for tc in range(c * per_half + per_half // 2, (c + 1) * per_half):
                build_p(blk, act_cur, p_cur, tc)

    stage(2 * s, va_ref, p_b, ua_ref, act_b, act_a, p_a)
    stage(2 * s + 1, vb_ref, p_a, ub_ref, act_a, act_b, p_b)

    @pl.when(s == pl.num_programs(1) - 1)
    def _():
        y = x_ref[...] + (acc_ref[...] + _dot(vl_ref[...], p_b[...])).T
        if final:
            ms = jnp.mean(y * y, axis=-1, keepdims=True)
            y = y * lax.rsqrt(ms + EPS) * fw_ref[...]
        o_ref[...] = y


def _peer(x, h2t, rank2, e2, n1, r1, u, vt, final_w, tt, eb, final):
    T, D = x.shape
    H, NK = PEER_HEADS, PEER_NKEYS
    ne = vt.shape[0]
    assert h2t.shape == (T // tt, D, tt) and vt.shape == (ne, D, eb) and u.shape == (ne * eb, D)
    sspec = pl.BlockSpec((H, NK, tt), lambda i, s: (0, 0, i))
    once = pl.Buffered(1)
    return pl.pallas_call(
        functools.partial(_peer_kernel, eb=eb, tt=tt, final=final),
        grid=(T // tt, ne // 2),
        in_specs=[pl.BlockSpec((tt, D), lambda i, s: (i, 0)),
                  pl.BlockSpec((None, D, tt), lambda i, s: (i, 0, 0)),
                  sspec, sspec, sspec, sspec,
                  pl.BlockSpec((eb, D), lambda i, s: (0, 0), pipeline_mode=once),
                  pl.BlockSpec((eb, D), lambda i, s: (2 * s + 1, 0)),
                  pl.BlockSpec((eb, D), lambda i, s: (jnp.minimum(2 * s + 2, ne - 1), 0)),
                  pl.BlockSpec((None, D, eb), lambda i, s: (jnp.maximum(2 * s - 1, 0), 0, 0)),
                  pl.BlockSpec((None, D, eb), lambda i, s: (2 * s, 0, 0)),
                  pl.BlockSpec((None, D, eb), lambda i, s: (ne - 1, 0, 0), pipeline_mode=once),
                  pl.BlockSpec((1, D), lambda i, s: (0, 0))],
        out_specs=pl.BlockSpec((tt, D), lambda i, s: (i, 0)),
        out_shape=jax.ShapeDtypeStruct((T, D), F32),
        scratch_shapes=[pltpu.VMEM((eb, tt), F32), pltpu.VMEM((eb, tt), F32),
                        pltpu.VMEM((eb, tt), BF16), pltpu.VMEM((eb, tt), BF16),
                        pltpu.VMEM((D, tt), F32)],
        compiler_params=_cparams(("parallel", "arbitrary")),
    )(x, h2t, rank2, e2, n1, r1, u, u, u, vt, vt, vt, final_w.reshape(1, D))


def _layer(x, batch, seq, lb, p, final_w, final, tiles):
    proj = _inproj(x, p["norm1_w"], p["w_in"], tiles["in_tm"], tiles["in_tn"])
    y_sb = _sb_attention(proj, batch, seq, tiles["sb_blk"], tiles["sb_nsub"])
    y_hg = _hgrn2(proj, lb, p["hg_norm_w"], batch, seq, tiles["hg_ct"])
    y_ssd = _ssd(proj, p["ssd_conv_w"], p["ssd_conv_b"], p["ssd_dt_bias"], p["ssd_a_log"],
                 p["ssd_d"], p["ssd_norm_w"], batch, seq)
    x, h2 = _merge(y_sb, y_hg, y_ssd, proj, x, p["w_branch_sb"], p["w_branch_hg"],
                   p["w_branch_ssd"], p["w_out"], p["norm2_w"], tiles["mg_tm"])
    s1t, s2t, h2t = _peer_query(h2, p["peer_w_q"], p["peer_sub_keys"], tiles["pq_tm"])
    rank2, e2, n1, r1 = _peer_topk(s1t, s2t, tiles["tk_tt"], tiles["tk_hb"])
    return _peer(x, h2t, rank2, e2, n1, r1, p["peer_u"], p["peer_vt"], final_w,
                 tiles["pe_tt"], tiles["pe_eb"], final)


TILES = dict(in_tm=2048, in_tn=768, sb_blk=128, sb_nsub=2, hg_ct=256, mg_tm=256, pq_tm=512,
             tk_tt=256, tk_hb=2, pe_tt=512, pe_eb=1024)


def _prep_w_in(w):
    c = np.cumsum([0, 512, 512, 512, 512, 512, 512, 512, 512, 768, 8, 1024, 1024, 1024])
    dt_rep = jnp.repeat(w[:, c[9]:c[10]], SSD_HEAD_DIM, axis=1)
    return jnp.concatenate([w[:, c[3]:c[7]], w[:, c[7]:c[8]], dt_rep, w[:, c[10]:c[13]],
                            w[:, c[8]:c[9]], w[:, c[0]:c[3]]], axis=1).astype(BF16)


def _trunk(x, norm1_w, w_in, hg_lb_logits, hg_norm_w, ssd_conv_w, ssd_conv_b, ssd_dt_bias,
           ssd_a_log, ssd_d, ssd_norm_w, w_branch_sb, w_branch_hg, w_branch_ssd, w_out,
           norm2_w, peer_w_q, peer_sub_keys, peer_u, peer_v, final_norm_w, tiles):
    batch, seq, d = x.shape
    depth = w_in.shape[0]
    gamma = jax.nn.softmax(hg_lb_logits.astype(F32), axis=0)
    lower_bounds = jnp.cumsum(gamma, axis=0) - gamma[0]
    xt = x.reshape(batch * seq, d)
    for l in range(depth):
        p = dict(
            norm1_w=norm1_w[l], w_in=_prep_w_in(w_in[l]), hg_norm_w=hg_norm_w[l],
            ssd_conv_w=ssd_conv_w[l], ssd_conv_b=ssd_conv_b[l], ssd_dt_bias=ssd_dt_bias[l],
            ssd_a_log=ssd_a_log[l], ssd_d=ssd_d[l], ssd_norm_w=ssd_norm_w[l],
            w_branch_sb=w_branch_sb[l].astype(BF16), w_branch_hg=w_branch_hg[l].astype(BF16),
            w_branch_ssd=w_branch_ssd[l].astype(BF16), w_out=w_out[l].astype(BF16),
            norm2_w=norm2_w[l], peer_w_q=peer_w_q[l].astype(BF16),
            peer_sub_keys=peer_sub_keys[l].reshape(2 * PEER_HEADS, PEER_NKEYS, -1).astype(BF16),
            peer_u=peer_u[l].astype(BF16),
            peer_vt=peer_v[l].reshape(-1, tiles["pe_eb"], d).transpose(0, 2, 1).astype(BF16))
        xt = _layer(xt, batch, seq, lower_bounds[l], p, final_norm_w, l == depth - 1, tiles)
    return xt.reshape(batch, seq, d)


def kernel(x, norm1_w, w_in, hg_lb_logits, hg_norm_w, ssd_conv_w, ssd_conv_b, ssd_dt_bias, ssd_a_log, ssd_d, ssd_norm_w, w_branch_sb, w_branch_hg, w_branch_ssd, w_out, norm2_w, peer_w_q, peer_sub_keys, peer_u, peer_v, final_norm_w):
    return _trunk(x, norm1_w, w_in, hg_lb_logits, hg_norm_w, ssd_conv_w, ssd_conv_b, ssd_dt_bias,
                  ssd_a_log, ssd_d, ssd_norm_w, w_branch_sb, w_branch_hg, w_branch_ssd, w_out,
                  norm2_w, peer_w_q, peer_sub_keys, peer_u, peer_v, final_norm_w, TILES)
```

```python
import functools

import numpy as np
import jax
import jax.numpy as jnp
from jax import lax
from jax.experimental import pallas as pl
from jax.experimental.pallas import tpu as pltpu

F32 = jnp.float32
BF16 = jnp.bfloat16

D_MODEL = 1024
SB_HEAD_DIM = 64
SB_WIDTH = 512
HG_HEADS = 4
HG_DK = 128
HG_CHUNK = 64
HG_F_FLOOR = 1e-30
SSD_HEADS = 8
SSD_HEAD_DIM = 64
SSD_WIDTH = 512
SSD_STATE = 64
SSD_CONV = 4
SSD_CHUNK = 128
SSD_CONV_DIM = 768
PEER_HEADS = 8
PEER_NKEYS = 128
PEER_TOPK = 16
EPS = 1e-6
NEG_BIG = -1e30

LANES = 128
SUBLANES = 8
VMEM_LIMIT = 56 * 1024 * 1024

COL_HG = 0
COL_Z = 2048
COL_DT = 2560
COL_GATE = 3072
COL_XBC = 6144
COL_SB = 6912
IN_COLS_PADDED = 8448


def _cparams(sem):
    return pltpu.CompilerParams(dimension_semantics=sem, vmem_limit_bytes=VMEM_LIMIT)


def _dot(a, b):
    return jnp.dot(a, b, preferred_element_type=F32)


def _dot_nt(a, b):
    return lax.dot_general(a, b, (((1,), (1,)), ((), ())), preferred_element_type=F32)


def _dot_tn(a, b):
    return lax.dot_general(a, b, (((0,), (0,)), ((), ())), preferred_element_type=F32)


def _sigmoid_pair(x):
    e = jnp.exp(-jnp.abs(x))
    inv = 1.0 / (1.0 + e)
    small = e * inv
    pos = x >= 0
    return jnp.where(pos, inv, small), jnp.where(pos, small, inv)


def _silu(x):
    return x * _sigmoid_pair(x)[0]


def _split3(x, axis):
    hi = x.astype(BF16)
    r1 = x - hi.astype(F32)
    mid = r1.astype(BF16)
    lo = (r1 - mid.astype(F32)).astype(BF16)
    return jnp.concatenate([hi, mid, lo], axis=axis)


def _inproj_kernel(x_ref, nw_ref, w_ref, o_ref, h_ref):
    @pl.when(pl.program_id(1) == 0)
    def _():
        x = x_ref[...]
        ms = jnp.mean(x * x, axis=-1, keepdims=True)
        h_ref[...] = (x * lax.rsqrt(ms + EPS) * nw_ref[...]).astype(BF16)

    o_ref[...] = _dot(h_ref[...], w_ref[...])


def _inproj(x, norm_w, w, tm, tn):
    T, D = x.shape
    N = w.shape[1]
    return pl.pallas_call(
        _inproj_kernel,
        grid=(T // tm, N // tn),
        in_specs=[
            pl.BlockSpec((tm, D), lambda i, j: (i, 0)),
            pl.BlockSpec((1, D), lambda i, j: (0, 0)),
            pl.BlockSpec((D, tn), lambda i, j: (0, j)),
        ],
        out_specs=pl.BlockSpec((tm, tn), lambda i, j: (i, j)),
        out_shape=jax.ShapeDtypeStruct((T, N), F32),
        scratch_shapes=[pltpu.VMEM((tm, D), BF16)],
        compiler_params=_cparams(("parallel", "arbitrary")),
    )(x, norm_w.reshape(1, D), w)


SB_EXIT = -110.0


def _sb_kernel(q_ref, k_ref, v_ref, mcat_ref, o_ref, c_ref, acc_ref, *, blk, nsub):
    nseq = q_ref.shape[0]
    qi = pl.program_id(1)
    lane = lax.broadcasted_iota(jnp.int32, (1, LANES), 1)
    first_head = lane < SB_HEAD_DIM
    mcat = mcat_ref[...]
    qs = []
    for g in range(nseq):
        qparts = []
        for s in range(nsub):
            q = q_ref[g, s * blk:(s + 1) * blk, :] * (SB_HEAD_DIM ** -0.5)
            qparts += [jnp.where(first_head, q, 0.0), jnp.where(first_head, 0.0, q)]
        qs.append(jnp.concatenate(qparts, axis=0).astype(BF16))
    c_ref[...] = jnp.zeros_like(c_ref)
    acc_ref[...] = jnp.zeros_like(acc_ref)

    def visit(kb, s0, diag):
        r0 = s0 * 2 * blk
        n = (nsub - s0) * 2 * blk
        ks = pl.multiple_of(kb * blk, blk)
        z = jnp.concatenate([_dot_nt(qs[g][r0:], k_ref[g, pl.ds(ks, blk), :].astype(BF16))
                             for g in range(nseq)], axis=0)
        t = jnp.log1p(jnp.exp(-jnp.abs(z)))
        log_beta = jnp.minimum(z, 0.0) - t
        log_rest = jnp.minimum(-z, 0.0) - t
        if diag:
            row = lax.rem(lax.broadcasted_iota(jnp.int32, z.shape, 0), n)
            col = lax.broadcasted_iota(jnp.int32, z.shape, 1)
            before = (row >= 2 * blk) | (col < (row & (blk - 1)))
            log_rest = jnp.where(before, log_rest, 0.0)
        hi = log_rest.astype(BF16)
        lo = (log_rest - hi.astype(F32)).astype(BF16)
        cs = _dot(jnp.concatenate([hi, lo], axis=1), mcat)
        carry = jnp.concatenate([c_ref[g, r0:, :] for g in range(nseq)], axis=0)
        a = jnp.exp(log_beta + cs[:, :blk] + carry)
        if diag:
            a = jnp.where(before, a, 0.0)
        carry = carry + cs[:, blk:]
        a = a.astype(BF16)
        for g in range(nseq):
            c_ref[g, r0:, :] = carry[g * n:(g + 1) * n]
            av = _dot(a[g * n:(g + 1) * n], v_ref[g, pl.ds(ks, blk), :].astype(BF16))
            for s in range(s0, nsub):
                i = (s - s0) * 2 * blk
                acc_ref[g, s * blk:(s + 1) * blk, :] += jnp.where(first_head, av[i:i + blk], av[i + blk:i + 2 * blk])
        return carry

    for j in range(nsub - 1, -1, -1):
        carry = visit(qi * nsub + j, j, True)

    def body(state):
        kb, _ = state
        return kb - 1, jnp.max(visit(kb, 0, False))

    lax.while_loop(lambda st: (st[0] >= 0) & (st[1] > SB_EXIT), body,
                   (qi * nsub - 1, jnp.max(carry)))
    o_ref[...] = acc_ref[...]


def _sb_consts(blk):
    r = np.arange(2 * blk)[:, None] % blk
    c = np.arange(2 * blk)[None, :]
    return jnp.asarray(((c >= blk) | (r > c)).astype(np.float32), dtype=BF16)


def _sb_attention(proj, batch, seq, blk, nsub):
    T, ncol = proj.shape
    tq = blk * nsub
    cb = COL_SB // LANES
    npair = SB_WIDTH // LANES
    proj3 = proj.reshape(batch, seq, ncol)
    out = pl.pallas_call(
        functools.partial(_sb_kernel, blk=blk, nsub=nsub),
        grid=(npair, seq // tq),
        in_specs=[
            pl.BlockSpec((batch, tq, LANES), lambda p, i: (0, i, cb + p)),
            pl.BlockSpec((batch, seq, LANES), lambda p, i: (0, 0, cb + npair + p)),
            pl.BlockSpec((batch, seq, LANES), lambda p, i: (0, 0, cb + 2 * npair + p)),
            pl.BlockSpec((2 * blk, 2 * blk), lambda p, i: (0, 0)),
        ],
        out_specs=pl.BlockSpec((batch, tq, LANES), lambda p, i: (0, i, p)),
        out_shape=jax.ShapeDtypeStruct((batch, seq, SB_WIDTH), F32),
        scratch_shapes=[pltpu.VMEM((batch, nsub * 2 * blk, blk), F32), pltpu.VMEM((batch, tq, LANES), F32)],
        compiler_params=_cparams(("parallel", "arbitrary")),
    )(proj3, proj3, proj3, _sb_consts(blk))
    return out.reshape(T, SB_WIDTH)


HG_LEVELS = (32, 16, 8, 4, 2, 1)


def _hg_consts():
    C = HG_CHUNK
    i = np.arange(C)
    sels = [i[None, :] <= i[:, None]]
    masks = []
    for m in HG_LEVELS:
        g = (i // (2 * m)) * (2 * m) + m - 1
        sels.append(i[None, :] <= g[:, None])
        same = (i[:, None] // (2 * m)) == (i[None, :] // (2 * m))
        upper_t = (i[:, None] % (2 * m)) >= m
        lower_s = (i[None, :] % (2 * m)) < m
        masks.append(same & upper_t & lower_s)
    masks.append(np.eye(C, dtype=bool))
    big = np.concatenate(sels, 0).astype(np.float32)
    big3 = np.concatenate([big, big, big], 1)
    return jnp.asarray(big3, dtype=BF16), jnp.asarray(np.stack(masks).astype(np.float32))


def _hg_kernel(f_ref, q_ref, i_ref, g_ref, lb_ref, nw_ref, sel_ref, msk_ref, o_ref, st_ref, *, nchunk):
    C = HG_CHUNK

    @pl.when(pl.program_id(1) == 0)
    def _():
        st_ref[...] = jnp.zeros_like(st_ref)

    sel = sel_ref[...]
    nw = nw_ref[...]
    lb = lb_ref[...]
    nlvl = len(HG_LEVELS)
    for c in range(nchunk):
        rows = slice(c * C, (c + 1) * C)
        sp, sn = _sigmoid_pair(f_ref[rows, :])
        f = lb + (1.0 - lb) * sp
        log_f = jnp.log(jnp.maximum(f, HG_F_FLOOR))
        key = (1.0 - lb) * sn
        q = _silu(q_ref[rows, :]) * (HG_DK ** -0.5)
        val = i_ref[rows, :].astype(BF16)
        br = _dot(sel, _split3(log_f, 0))
        b = br[:C]
        qk = []
        for lvl in range(nlvl):
            r = br[(lvl + 1) * C:(lvl + 2) * C]
            qk.append(((q * jnp.exp(jnp.minimum(b - r, 0.0))).astype(BF16),
                       (key * jnp.exp(jnp.minimum(r - b, 0.0))).astype(BF16)))
        qk.append((q.astype(BF16), key.astype(BF16)))
        q_in = (q * jnp.exp(b)).astype(BF16)
        b_last = b[C - 1:C, :]
        kt = (key * jnp.exp(b_last - b)).astype(BF16)
        decay = jnp.exp(b_last)
        outs = []
        for h in range(HG_HEADS):
            cols = slice(h * HG_DK, (h + 1) * HG_DK)
            s = msk_ref[0] * _dot_nt(qk[0][0][:, cols], qk[0][1][:, cols])
            for lvl in range(1, nlvl + 1):
                s = s + msk_ref[lvl] * _dot_nt(qk[lvl][0][:, cols], qk[lvl][1][:, cols])
            st = st_ref[h]
            o = _dot(s.astype(BF16), val[:, cols]) + _dot_nt(q_in[:, cols], st.astype(BF16))
            st_ref[h] = st * decay[:, cols] + _dot_tn(val[:, cols], kt[:, cols])
            ms = jnp.mean(o * o, axis=-1, keepdims=True)
            outs.append(o * lax.rsqrt(ms + EPS) * nw)
        o_ref[rows, :] = jnp.concatenate(outs, axis=1) * _silu(g_ref[rows, :])


def _hgrn2(proj, lb, norm_w, batch, seq, ct):
    T = proj.shape[0]
    W = HG_HEADS * HG_DK
    nc = seq // ct
    sel, msk = _hg_consts()
    cb = COL_HG // W
    spec = lambda k: pl.BlockSpec((ct, W), lambda b, c: (b * nc + c, cb + k))
    return pl.pallas_call(
        functools.partial(_hg_kernel, nchunk=ct // HG_CHUNK),
        grid=(batch, nc),
        in_specs=[spec(0), spec(1), spec(2), spec(3),
                  pl.BlockSpec((1, W), lambda b, c: (0, 0)),
                  pl.BlockSpec((1, HG_DK), lambda b, c: (0, 0)),
                  pl.BlockSpec(sel.shape, lambda b, c: (0, 0)),
                  pl.BlockSpec(msk.shape, lambda b, c: (0, 0, 0))],
        out_specs=pl.BlockSpec((ct, W), lambda b, c: (b * nc + c, 0)),
        out_shape=jax.ShapeDtypeStruct((T, W), F32),
        scratch_shapes=[pltpu.VMEM((HG_HEADS, HG_DK, HG_DK), F32)],
        compiler_params=_cparams(("parallel", "arbitrary")),
    )(proj, proj, proj, proj, lb.reshape(1, W), norm_w.reshape(1, HG_DK), sel, msk)


SSD_PAD = 8


def _ssd_kernel(z_ref, xbc_ref, dt_ref, cw_ref, cb_ref, dtb_ref, a_ref, d_ref, nw_ref, tril3_ref,
                o_ref, pad_ref, st_ref):
    L = SSD_CHUNK
    W = SSD_WIDTH
    GN = 2 * SSD_STATE

    @pl.when(pl.program_id(1) == 0)
    def _():
        pad_ref[0:SSD_PAD, :] = jnp.zeros((SSD_PAD, SSD_CONV_DIM), F32)
        st_ref[...] = jnp.zeros_like(st_ref)

    pad_ref[SSD_PAD:SSD_PAD + L, :] = xbc_ref[...]
    conv = cb_ref[...]
    for k in range(SSD_CONV):
        off = SSD_PAD - (SSD_CONV - 1) + k
        conv = conv + cw_ref[k:k + 1, :] * pad_ref[off:off + L, :]
    pad_ref[0:SSD_PAD, :] = pad_ref[L:L + SSD_PAD, :]
    xa = _silu(conv)
    xs = xa[:, :W]
    bm = xa[:, W:W + GN]
    cm = xa[:, W + GN:W + 2 * GN]
    dtv = dt_ref[...] + dtb_ref[...]
    dt = jnp.maximum(dtv, 0.0) + jnp.log1p(jnp.exp(-jnp.abs(dtv)))
    a = a_ref[...] * dt
    tril3 = tril3_ref[...]
    a_cum = _dot(tril3, _split3(a, 0))
    a_last = a_cum[L - 1:L, :]
    xdt = xs * dt

    lane = lax.broadcasted_iota(jnp.int32, (1, LANES), 1)
    half = [lane < SSD_STATE, lane >= SSD_STATE]
    row = lax.broadcasted_iota(jnp.int32, (L, L), 0)
    col = lax.broadcasted_iota(jnp.int32, (L, L), 1)
    strict = col < row
    causal = col <= row
    bb = bm.astype(BF16)
    cbm = cm.astype(BF16)
    cb_g = [_dot_nt(jnp.where(half[g], cm, 0.0).astype(BF16), bb) for g in range(2)]

    y_parts = []
    for pair in range(SSD_HEADS // 2):
        g = pair // 2
        xp = xdt[:, pair * LANES:(pair + 1) * LANES]
        acc = jnp.zeros((L, LANES), F32)
        for hh in range(2):
            h = 2 * pair + hh
            a_col = a[:, h * SSD_HEAD_DIM:h * SSD_HEAD_DIM + 1]
            diff = _dot(tril3, _split3(jnp.where(strict, a_col, 0.0), 0))
            seg = jnp.exp(jnp.where(causal, diff, NEG_BIG))
            sc = (cb_g[g] * seg).astype(BF16)
            acc = acc + _dot(sc, jnp.where(half[hh], xp, 0.0).astype(BF16))
        y_parts.append(acc)
    y = jnp.concatenate(y_parts, axis=1)

    st = st_ref[...]
    y = y + jnp.exp(a_cum) * _dot(cbm, st.astype(BF16))
    srow = lax.broadcasted_iota(jnp.int32, (GN, W), 0) // SSD_STATE
    scol = lax.broadcasted_iota(jnp.int32, (GN, W), 1) // (W // 2)
    upd = _dot_tn(bb, (jnp.exp(a_last - a_cum) * xdt).astype(BF16))
    st_ref[...] = st * jnp.exp(a_last) + jnp.where(srow == scol, upd, 0.0)

    y = y + d_ref[...] * xs
    y = y * _silu(z_ref[...])
    nw = nw_ref[...]
    outs = []
    for g in range(2):
        yg = y[:, g * (W // 2):(g + 1) * (W // 2)]
        ms = jnp.mean(yg * yg, axis=-1, keepdims=True)
        outs.append(yg * lax.rsqrt(ms + EPS) * nw[:, g * (W // 2):(g + 1) * (W // 2)])
    o_ref[...] = jnp.concatenate(outs, axis=1)


def _ssd(proj, conv_w, conv_b, dt_bias, a_log, d_skip, norm_w, batch, seq):
    T = proj.shape[0]
    L = SSD_CHUNK
    W = SSD_WIDTH
    nc = seq // L
    rep = lambda v: jnp.repeat(v.astype(F32), SSD_HEAD_DIM).reshape(1, W)
    tril = np.tril(np.ones((L, L), np.float32))
    tril3 = jnp.asarray(np.concatenate([tril, tril, tril], 1), dtype=BF16)
    vec = lambda n: pl.BlockSpec((1, n), lambda b, c: (0, 0))
    return pl.pallas_call(
        _ssd_kernel,
        grid=(batch, nc),
        in_specs=[
            pl.BlockSpec((L, W), lambda b, c: (b * nc + c, COL_Z // W)),
            pl.BlockSpec((L, SSD_CONV_DIM), lambda b, c: (b * nc + c, COL_XBC // SSD_CONV_DIM)),
            pl.BlockSpec((L, W), lambda b, c: (b * nc + c, COL_DT // W)),
            pl.BlockSpec((SSD_CONV, SSD_CONV_DIM), lambda b, c: (0, 0)),
            vec(SSD_CONV_DIM), vec(W), vec(W), vec(W), vec(W),
            pl.BlockSpec((L, 3 * L), lambda b, c: (0, 0)),
        ],
        out_specs=pl.BlockSpec((L, W), lambda b, c: (b * nc + c, 0)),
        out_shape=jax.ShapeDtypeStruct((T, W), F32),
        scratch_shapes=[pltpu.VMEM((L + SSD_PAD, SSD_CONV_DIM), F32),
                        pltpu.VMEM((2 * SSD_STATE, W), F32)],
        compiler_params=_cparams(("parallel", "arbitrary")),
    )(proj, proj, proj, conv_w, conv_b.reshape(1, SSD_CONV_DIM), rep(dt_bias),
      rep(-jnp.exp(a_log.astype(F32))), rep(d_skip), norm_w.reshape(1, W), tril3)


def _merge_kernel(ysb_ref, yhg_ref, yssd_ref, g0_ref, g1_ref, g2_ref, x_ref,
                  w0_ref, w1_ref, w2_ref, wo_ref, nw_ref, xo_ref, h_ref):
    m = _sigmoid_pair(g0_ref[...])[0] * _dot(ysb_ref[...].astype(BF16), w0_ref[...])
    m = m + _sigmoid_pair(g1_ref[...])[0] * _dot(yhg_ref[...].astype(BF16), w1_ref[...])
    m = m + _sigmoid_pair(g2_ref[...])[0] * _dot(yssd_ref[...].astype(BF16), w2_ref[...])
    xn = x_ref[...] + _dot(m.astype(BF16), wo_ref[...])
    xo_ref[...] = xn
    ms = jnp.mean(xn * xn, axis=-1, keepdims=True)
    h_ref[...] = (xn * lax.rsqrt(ms + EPS) * nw_ref[...]).astype(BF16)


def _merge(y_sb, y_hg, y_ssd, proj, x, w_sb, w_hg, w_ssd, w_out, norm_w, tm):
    T, D = x.shape
    Wb = y_sb.shape[1]
    yspec = pl.BlockSpec((tm, Wb), lambda i: (i, 0))
    gspec = lambda k: pl.BlockSpec((tm, D), lambda i: (i, COL_GATE // D + k))
    wspec = pl.BlockSpec((Wb, D), lambda i: (0, 0))
    xspec = pl.BlockSpec((tm, D), lambda i: (i, 0))
    return pl.pallas_call(
        _merge_kernel,
        grid=(T // tm,),
        in_specs=[yspec, yspec, yspec, gspec(0), gspec(1), gspec(2), xspec,
                  wspec, wspec, wspec,
                  pl.BlockSpec((D, D), lambda i: (0, 0)),
                  pl.BlockSpec((1, D), lambda i: (0, 0))],
        out_specs=[xspec, xspec],
        out_shape=[jax.ShapeDtypeStruct((T, D), F32), jax.ShapeDtypeStruct((T, D), BF16)],
        compiler_params=_cparams(("parallel",)),
    )(y_sb, y_hg, y_ssd, proj, proj, proj, x, w_sb, w_hg, w_ssd, w_out, norm_w.reshape(1, D))


def _peerq_kernel(h_ref, wq_ref, sk_ref, s1_ref, s2_ref, ht_ref):
    h = h_ref[...]
    q = _dot(h, wq_ref[...]).astype(BF16)
    for hd in range(PEER_HEADS):
        for p, s_ref in enumerate((s1_ref, s2_ref)):
            j = hd * 2 + p
            s_ref[hd] = _dot_nt(sk_ref[j], q[:, j * PEER_NKEYS:(j + 1) * PEER_NKEYS])
    ht_ref[...] = h.astype(F32).T.astype(BF16)


def _peer_query(h2, w_q, sub_keys, tm):
    T, D = h2.shape
    NK = PEER_NKEYS
    sspec = pl.BlockSpec((PEER_HEADS, NK, tm), lambda i: (0, 0, i))
    return pl.pallas_call(
        _peerq_kernel,
        grid=(T // tm,),
        in_specs=[pl.BlockSpec((tm, D), lambda i: (i, 0)),
                  pl.BlockSpec(w_q.shape, lambda i: (0, 0)),
                  pl.BlockSpec(sub_keys.shape, lambda i: (0, 0, 0))],
        out_specs=[sspec, sspec, pl.BlockSpec((None, D, tm), lambda i: (i, 0, 0))],
        out_shape=[jax.ShapeDtypeStruct((PEER_HEADS, NK, T), F32),
                   jax.ShapeDtypeStruct((PEER_HEADS, NK, T), F32),
                   jax.ShapeDtypeStruct((T // tm, D, tm), BF16)],
        compiler_params=_cparams(("parallel",)),
    )(h2, w_q, sub_keys)


def _oddeven_merge(lo, hi, r):
    step = r * 2
    if step < hi - lo:
        yield from _oddeven_merge(lo, hi, step)
        yield from _oddeven_merge(lo + r, hi, step)
        yield from [(i, i + r) for i in range(lo + r, hi - r, step)]
    else:
        yield (lo, lo + r)


def _oddeven_sort(lo, hi):
    if hi - lo >= 1:
        mid = lo + (hi - lo) // 2
        yield from _oddeven_sort(lo, mid)
        yield from _oddeven_sort(mid + 1, hi)
        yield from _oddeven_merge(lo, hi, 1)


SORT16 = tuple(_oddeven_sort(0, PEER_TOPK - 1))


def _cmp_exchange(x, i, j):
    x[i], x[j] = jnp.maximum(x[i], x[j]), jnp.minimum(x[i], x[j])


def _top16(x, n_valid):
    K = PEER_TOPK
    for i, j in SORT16:
        if j < n_valid:
            _cmp_exchange(x, i, j)
    for shift in (4, 2, 1):
        y = [pltpu.roll(v, shift, 0) for v in x]
        x = [jnp.maximum(x[i], y[K - 1 - i]) for i in range(K)]
        d = K // 2
        while d >= 1:
            for i in range(K):
                if i & d == 0:
                    _cmp_exchange(x, i, i + d)
            d //= 2
    return x


def _pack_rows(rep, sub):
    out = rep[0]
    for i in range(1, SUBLANES):
        out = jnp.where(sub == i, rep[i], out)
    return out


def _peer_topk_kernel(s1_ref, s2_ref, rank_ref, e2_ref, n_ref, r1_ref):
    for hd in range(s1_ref.shape[0]):
        _peer_topk_head(hd, s1_ref, s2_ref, rank_ref, e2_ref, n_ref, r1_ref)


def _peer_topk_head(hd, s1_ref, s2_ref, rank_ref, e2_ref, n_ref, r1_ref):
    K = PEER_TOPK
    G = PEER_NKEYS // SUBLANES
    tt = s1_ref.shape[2]
    s1 = [s1_ref[hd, g * SUBLANES:(g + 1) * SUBLANES, :] for g in range(G)]
    s2 = [s2_ref[hd, g * SUBLANES:(g + 1) * SUBLANES, :] for g in range(G)]
    v1 = _top16(list(s1), G)
    v2 = _top16(list(s2), G)
    sub = lax.broadcasted_iota(jnp.int32, (SUBLANES, tt), 0)
    v1p = [_pack_rows(v1[:SUBLANES], sub), _pack_rows(v1[SUBLANES:], sub)]
    v2p_hi = _pack_rows(v2[SUBLANES:], sub)
    cand = [v1p[0] + v2[0], v1p[1] + v2[0]] + [v1p[0] + v2[j] for j in range(1, SUBLANES)]
    cand.append(v1[0] + v2p_hi)
    ninf = jnp.full((SUBLANES, tt), -jnp.inf, F32)
    top = _top16(cand + [ninf] * (K - len(cand)), len(cand))
    tau = top[K - 1]
    zsum = jnp.ones_like(tau)
    for r in range(1, K):
        zsum = zsum + jnp.exp(top[r] - top[0])
    cnt = [jnp.zeros((SUBLANES, tt), F32), jnp.zeros((SUBLANES, tt), F32)]
    for j in range(K):
        for k in range(2):
            cnt[k] = cnt[k] + jnp.where(v1p[k] + v2[j] >= tau, 1.0, 0.0)
    n_rep = [jnp.broadcast_to(cnt[i // SUBLANES][i % SUBLANES:i % SUBLANES + 1, :], (SUBLANES, tt))
             for i in range(K)]
    n_rep = [n_rep[0]] + [jnp.where(v1[i] < v1[i - 1], n_rep[i], 0.0) for i in range(1, K)]
    inv_z = 1.0 / zsum
    for g in range(G):
        rows = slice(g * SUBLANES, (g + 1) * SUBLANES)
        n = jnp.zeros((SUBLANES, tt), F32)
        for i in range(K - 1, -1, -1):
            n = jnp.where(s1[g] == v1[i], n_rep[i], n)
        n_ref[hd, rows, :] = n
        r1_ref[hd, rows, :] = jnp.exp(s1[g] - v1[0]) * inv_z
    for g in range(0, G, 2):
        rk, e2 = [], []
        for gg in (g, g + 1):
            r = jnp.zeros((SUBLANES, tt), F32)
            for j in range(K):
                r = r + jnp.where(v2[j] > s2[gg], 1.0, 0.0)
            rk.append(r)
            e2.append(jnp.exp(s2[gg] - v2[0]))
        rows = slice(g * SUBLANES, (g + 2) * SUBLANES)
        rank_ref[hd, rows, :] = jnp.concatenate(rk, axis=0).astype(BF16)
        e2_ref[hd, rows, :] = jnp.concatenate(e2, axis=0).astype(BF16)


def _peer_topk(s1t, s2t, tt, hb):
    H, NK, T = s1t.shape
    sspec = pl.BlockSpec((hb, NK, tt), lambda i, h: (h, 0, i))
    return pl.pallas_call(
        _peer_topk_kernel,
        grid=(T // tt, H // hb),
        in_specs=[sspec, sspec],
        out_specs=[sspec, sspec, sspec, sspec],
        out_shape=[jax.ShapeDtypeStruct((H, NK, T), BF16), jax.ShapeDtypeStruct((H, NK, T), BF16),
                   jax.ShapeDtypeStruct((H, NK, T), F32), jax.ShapeDtypeStruct((H, NK, T), F32)],
        compiler_params=_cparams(("parallel", "parallel")),
    )(s1t, s2t)


def _peer_kernel(x_ref, ht_ref, htn_ref, rank_ref, e2_ref, n_ref, r1_ref, u0_ref, u1_ref, ua_ref, ub_ref,
                 va_ref, vb_ref, fw_ref, o_ref, act_a, act_b, p_a, p_b, acc_ref, *, eb, tt, final):
    s = pl.program_id(1)
    NK = PEER_NKEYS
    assert eb // NK == SUBLANES
    zero = jnp.zeros((), BF16)

    def build_p(blk, act_ref, p_ref, tc):
        a0 = pl.multiple_of(blk * SUBLANES, SUBLANES)
        ts = slice(tc * LANES, (tc + 1) * LANES)
        n8 = [n_ref[h, pl.ds(a0, SUBLANES), ts] for h in range(PEER_HEADS)]
        r8 = [r1_ref[h, pl.ds(a0, SUBLANES), ts] for h in range(PEER_HEADS)]
        for ai in range(SUBLANES):
            w = None
            for h in range(PEER_HEADS):
                n_row = n8[h][ai:ai + 1, :].astype(BF16)
                r_row = r8[h][ai:ai + 1, :].astype(BF16)
                term = jnp.where(rank_ref[h, :, ts] < n_row, e2_ref[h, :, ts] * r_row, zero)
                w = term if w is None else w + term
            act = act_ref[ai * NK:(ai + 1) * NK, ts]
            gelu = 0.5 * act * (1.0 + lax.erf(act * (2.0 ** -0.5)))
            p_ref[ai * NK:(ai + 1) * NK, ts] = w * gelu.astype(BF16)

    @pl.when(s == 0)
    def _():
        acc_ref[...] = jnp.zeros_like(acc_ref)

    @pl.when((s == 0) & (pl.program_id(0) == 0))
    def _():
        act_a[...] = _dot(u0_ref[...], ht_ref[...])
        act_b[...] = _dot(u1_ref[...], ht_ref[...])

    half = tt // 2
    per_half = half // LANES

    def stage(blk, act_ref, p_ref, v_ref, u_next_ref):
        for c in range(2):
            cs = slice(c * half, (c + 1) * half)
            for tc in range(c * per_half, (c + 1) * per_half):
                build_p(blk, act_ref, p_ref, tc)
            acc_ref[:, cs] += _dot(v_ref[...], p_ref[:, cs])
            act_ref[:, cs] = _dot(u_next_ref[...], htn_ref[:, cs])

    stage(2 * s, act_a, p_a, va_ref, ua_ref)
    stage(2 * s + 1, act_b, p_b, vb_ref, ub_ref)

    @pl.when(s == pl.num_programs(1) - 1)
    def _():
        y = x_ref[...] + acc_ref[...].T
        if final:
            ms = jnp.mean(y * y, axis=-1, keepdims=True)
            y = y * lax.rsqrt(ms + EPS) * fw_ref[...]
        o_ref[...] = y


def _peer(x, h2t, rank2, e2, n1, r1, u, vt, final_w, tt, eb, final):
    T, D = x.shape
    H, NK = PEER_HEADS, PEER_NKEYS
    ne = vt.shape[0]
    assert h2t.shape == (T // tt, D, tt) and vt.shape == (ne, D, eb) and u.shape == (ne * eb, D)
    sspec = pl.BlockSpec((H, NK, tt), lambda i, s: (0, 0, i))
    once = pl.Buffered(1)
    nt, ns = T // tt, ne // 2
    return pl.pallas_call(
        functools.partial(_peer_kernel, eb=eb, tt=tt, final=final),
        grid=(nt, ns),
        in_specs=[pl.BlockSpec((tt, D), lambda i, s: (i, 0)),
                  pl.BlockSpec((None, D, tt), lambda i, s: (i, 0, 0)),
                  pl.BlockSpec((None, D, tt), lambda i, s: (jnp.minimum(i + (s + 1) // ns, nt - 1), 0, 0)),
                  sspec, sspec, sspec, sspec,
                  pl.BlockSpec((eb, D), lambda i, s: (0, 0), pipeline_mode=once),
                  pl.BlockSpec((eb, D), lambda i, s: (1, 0), pipeline_mode=once),
                  pl.BlockSpec((eb, D), lambda i, s: ((2 * s + 2) % ne, 0)),
                  pl.BlockSpec((eb, D), lambda i, s: ((2 * s + 3) % ne, 0)),
                  pl.BlockSpec((None, D, eb), lambda i, s: (2 * s, 0, 0)),
                  pl.BlockSpec((None, D, eb), lambda i, s: (2 * s + 1, 0, 0)),
                  pl.BlockSpec((1, D), lambda i, s: (0, 0))],
        out_specs=pl.BlockSpec((tt, D), lambda i, s: (i, 0)),
        out_shape=jax.ShapeDtypeStruct((T, D), F32),
        scratch_shapes=[pltpu.VMEM((eb, tt), F32), pltpu.VMEM((eb, tt), F32),
                        pltpu.VMEM((eb, tt), BF16), pltpu.VMEM((eb, tt), BF16),
                        pltpu.VMEM((D, tt), F32)],
        compiler_params=_cparams(("arbitrary", "arbitrary")),
    )(x, h2t, h2t, rank2, e2, n1, r1, u, u, u, u, vt, vt, final_w.reshape(1, D))


def _layer(x, batch, seq, lb, p, final_w, final, tiles):
    proj = _inproj(x, p["norm1_w"], p["w_in"], tiles["in_tm"], tiles["in_tn"])
    y_sb = _sb_attention(proj, batch, seq, tiles["sb_blk"], tiles["sb_nsub"])
    y_hg = _hgrn2(proj, lb, p["hg_norm_w"], batch, seq, tiles["hg_ct"])
    y_ssd = _ssd(proj, p["ssd_conv_w"], p["ssd_conv_b"], p["ssd_dt_bias"], p["ssd_a_log"],
                 p["ssd_d"], p["ssd_norm_w"], batch, seq)
    x, h2 = _merge(y_sb, y_hg, y_ssd, proj, x, p["w_branch_sb"], p["w_branch_hg"],
                   p["w_branch_ssd"], p["w_out"], p["norm2_w"], tiles["mg_tm"])
    s1t, s2t, h2t = _peer_query(h2, p["peer_w_q"], p["peer_sub_keys"], tiles["pq_tm"])
    rank2, e2, n1, r1 = _peer_topk(s1t, s2t, tiles["tk_tt"], tiles["tk_hb"])
    return _peer(x, h2t, rank2, e2, n1, r1, p["peer_u"], p["peer_vt"], final_w,
                 tiles["pe_tt"], tiles["pe_eb"], final)


TILES = dict(in_tm=2048, in_tn=768, sb_blk=128, sb_nsub=2, hg_ct=256, mg_tm=256, pq_tm=512,
             tk_tt=256, tk_hb=2, pe_tt=512, pe_eb=1024)


def _prep_w_in(w):
    c = np.cumsum([0, 512, 512, 512, 512, 512, 512, 512, 512, 768, 8, 1024, 1024, 1024])
    dt_rep = jnp.repeat(w[:, c[9]:c[10]], SSD_HEAD_DIM, axis=1)
    return jnp.concatenate([w[:, c[3]:c[7]], w[:, c[7]:c[8]], dt_rep, w[:, c[10]:c[13]],
                            w[:, c[8]:c[9]], w[:, c[0]:c[3]]], axis=1).astype(BF16)


def _trunk(x, norm1_w, w_in, hg_lb_logits, hg_norm_w, ssd_conv_w, ssd_conv_b, ssd_dt_bias,
           ssd_a_log, ssd_d, ssd_norm_w, w_branch_sb, w_branch_hg, w_branch_ssd, w_out,
           norm2_w, peer_w_q, peer_sub_keys, peer_u, peer_v, final_norm_w, tiles):
    batch, seq, d = x.shape
    depth = w_in.shape[0]
    gamma = jax.nn.softmax(hg_lb_logits.astype(F32), axis=0)
    lower_bounds = jnp.cumsum(gamma, axis=0) - gamma[0]
    xt = x.reshape(batch * seq, d)
    for l in range(depth):
        p = dict(
            norm1_w=norm1_w[l], w_in=_prep_w_in(w_in[l]), hg_norm_w=hg_norm_w[l],
            ssd_conv_w=ssd_conv_w[l], ssd_conv_b=ssd_conv_b[l], ssd_dt_bias=ssd_dt_bias[l],
            ssd_a_log=ssd_a_log[l], ssd_d=ssd_d[l], ssd_norm_w=ssd_norm_w[l],
            w_branch_sb=w_branch_sb[l].astype(BF16), w_branch_hg=w_branch_hg[l].astype(BF16),
            w_branch_ssd=w_branch_ssd[l].astype(BF16), w_out=w_out[l].astype(BF16),
            norm2_w=norm2_w[l], peer_w_q=peer_w_q[l].astype(BF16),
            peer_sub_keys=peer_sub_keys[l].reshape(2 * PEER_HEADS, PEER_NKEYS, -1).astype(BF16),
            peer_u=peer_u[l].astype(BF16),
            peer_vt=peer_v[l].reshape(-1, tiles["pe_eb"], d).transpose(0, 2, 1).astype(BF16))
        xt = _layer(xt, batch, seq, lower_bounds[l], p, final_norm_w, l == depth - 1, tiles)
    return xt.reshape(batch, seq, d)


def kernel(x, norm1_w, w_in, hg_lb_logits, hg_norm_w, ssd_conv_w, ssd_conv_b, ssd_dt_bias, ssd_a_log, ssd_d, ssd_norm_w, w_branch_sb, w_branch_hg, w_branch_ssd, w_out, norm2_w, peer_w_q, peer_sub_keys, peer_u, peer_v, final_norm_w):
    return _trunk(x, norm1_w, w_in, hg_lb_logits, hg_norm_w, ssd_conv_w, ssd_conv_b, ssd_dt_bias,
                  ssd_a_log, ssd_d, ssd_norm_w, w_branch_sb, w_branch_hg, w_branch_ssd, w_out,
                  norm2_w, peer_w_q, peer_sub_keys, peer_u, peer_v, final_norm_w, TILES)
```

```python
import functools

import numpy as np
import jax
import jax.numpy as jnp
from jax import lax
from jax.experimental import pallas as pl
from jax.experimental.pallas import tpu as pltpu

F32 = jnp.float32
BF16 = jnp.bfloat16
BRANCH_DTYPE = BF16

D_MODEL = 1024
SB_HEAD_DIM = 64
SB_WIDTH = 512
HG_HEADS = 4
HG_DK = 128
HG_CHUNK = 64
HG_F_FLOOR = 1e-30
SSD_HEADS = 8
SSD_HEAD_DIM = 64
SSD_WIDTH = 512
SSD_STATE = 64
SSD_CONV = 4
SSD_CHUNK = 128
SSD_CONV_DIM = 768
PEER_HEADS = 8
PEER_NKEYS = 128
PEER_TOPK = 16
EPS = 1e-6
NEG_BIG = -1e30

LANES = 128
SUBLANES = 8
VMEM_LIMIT = 56 * 1024 * 1024

COL_HG = 0
COL_Z = 2048
COL_DT = 2560
COL_GATE = 3072
COL_XBC = 6144
COL_SB = 6912
IN_COLS_PADDED = 8448


def _cparams(sem):
    return pltpu.CompilerParams(dimension_semantics=sem, vmem_limit_bytes=VMEM_LIMIT)


def _dot(a, b):
    return jnp.dot(a, b, preferred_element_type=F32)


def _dot_nt(a, b):
    return lax.dot_general(a, b, (((1,), (1,)), ((), ())), preferred_element_type=F32)


def _dot_tn(a, b):
    return lax.dot_general(a, b, (((0,), (0,)), ((), ())), preferred_element_type=F32)


def _sigmoid_pair(x):
    e = jnp.exp(-jnp.abs(x))
    inv = 1.0 / (1.0 + e)
    small = e * inv
    pos = x >= 0
    return jnp.where(pos, inv, small), jnp.where(pos, small, inv)


def _silu(x):
    return x * _sigmoid_pair(x)[0]


def _split3(x, axis):
    hi = x.astype(BF16)
    r1 = x - hi.astype(F32)
    mid = r1.astype(BF16)
    lo = (r1 - mid.astype(F32)).astype(BF16)
    return jnp.concatenate([hi, mid, lo], axis=axis)


def _inproj_kernel(x_ref, nw_ref, w_ref, o_ref, h_ref):
    @pl.when(pl.program_id(1) == 0)
    def _():
        x = x_ref[...]
        ms = jnp.mean(x * x, axis=-1, keepdims=True)
        h_ref[...] = (x * lax.rsqrt(ms + EPS) * nw_ref[...]).astype(BF16)

    o_ref[...] = _dot(h_ref[...], w_ref[...])


def _inproj(x, norm_w, w, tm, tn):
    T, D = x.shape
    N = w.shape[1]
    return pl.pallas_call(
        _inproj_kernel,
        grid=(T // tm, N // tn),
        in_specs=[
            pl.BlockSpec((tm, D), lambda i, j: (i, 0)),
            pl.BlockSpec((1, D), lambda i, j: (0, 0)),
            pl.BlockSpec((D, tn), lambda i, j: (0, j)),
        ],
        out_specs=pl.BlockSpec((tm, tn), lambda i, j: (i, j)),
        out_shape=jax.ShapeDtypeStruct((T, N), F32),
        scratch_shapes=[pltpu.VMEM((tm, D), BF16)],
        compiler_params=_cparams(("parallel", "arbitrary")),
    )(x, norm_w.reshape(1, D), w)


SB_EXIT = -110.0


def _sb_kernel(q_ref, k_ref, v_ref, mcat_ref, o_ref, c_ref, acc_ref, *, blk, nsub):
    nseq = q_ref.shape[0]
    qi = pl.program_id(1)
    lane = lax.broadcasted_iota(jnp.int32, (1, LANES), 1)
    first_head = lane < SB_HEAD_DIM
    mcat = mcat_ref[...]
    qs = []
    for g in range(nseq):
        qparts = []
        for s in range(nsub):
            q = q_ref[g, s * blk:(s + 1) * blk, :] * (SB_HEAD_DIM ** -0.5)
            qparts += [jnp.where(first_head, q, 0.0), jnp.where(first_head, 0.0, q)]
        qs.append(jnp.concatenate(qparts, axis=0).astype(BF16))
    c_ref[...] = jnp.zeros_like(c_ref)
    acc_ref[...] = jnp.zeros_like(acc_ref)

    def visit(kb, s0, diag):
        r0 = s0 * 2 * blk
        n = (nsub - s0) * 2 * blk
        ks = pl.multiple_of(kb * blk, blk)
        z = jnp.concatenate([_dot_nt(qs[g][r0:], k_ref[g, pl.ds(ks, blk), :].astype(BF16))
                             for g in range(nseq)], axis=0)
        t = jnp.log1p(jnp.exp(-jnp.abs(z)))
        log_beta = jnp.minimum(z, 0.0) - t
        log_rest = jnp.minimum(-z, 0.0) - t
        if diag:
            row = lax.rem(lax.broadcasted_iota(jnp.int32, z.shape, 0), n)
            col = lax.broadcasted_iota(jnp.int32, z.shape, 1)
            before = (row >= 2 * blk) | (col < (row & (blk - 1)))
            log_rest = jnp.where(before, log_rest, 0.0)
        hi = log_rest.astype(BF16)
        lo = (log_rest - hi.astype(F32)).astype(BF16)
        cs = _dot(jnp.concatenate([hi, lo], axis=1), mcat)
        carry = jnp.concatenate([c_ref[g, r0:, :] for g in range(nseq)], axis=0)
        a = jnp.exp(log_beta + cs[:, :blk] + carry)
        if diag:
            a = jnp.where(before, a, 0.0)
        carry = carry + cs[:, blk:]
        a = a.astype(BF16)
        for g in range(nseq):
            c_ref[g, r0:, :] = carry[g * n:(g + 1) * n]
            av = _dot(a[g * n:(g + 1) * n], v_ref[g, pl.ds(ks, blk), :].astype(BF16))
            for s in range(s0, nsub):
                i = (s - s0) * 2 * blk
                acc_ref[g, s * blk:(s + 1) * blk, :] += jnp.where(first_head, av[i:i + blk], av[i + blk:i + 2 * blk])
        return carry

    for j in range(nsub - 1, -1, -1):
        carry = visit(qi * nsub + j, j, True)

    def body(state):
        kb, _ = state
        return kb - 1, jnp.max(visit(kb, 0, False))

    lax.while_loop(lambda st: (st[0] >= 0) & (st[1] > SB_EXIT), body,
                   (qi * nsub - 1, jnp.max(carry)))
    o_ref[...] = acc_ref[...].astype(o_ref.dtype)


def _sb_consts(blk):
    r = np.arange(2 * blk)[:, None] % blk
    c = np.arange(2 * blk)[None, :]
    return jnp.asarray(((c >= blk) | (r > c)).astype(np.float32), dtype=BF16)


def _sb_attention(proj, batch, seq, blk, nsub):
    T, ncol = proj.shape
    tq = blk * nsub
    cb = COL_SB // LANES
    npair = SB_WIDTH // LANES
    proj3 = proj.reshape(batch, seq, ncol)
    out = pl.pallas_call(
        functools.partial(_sb_kernel, blk=blk, nsub=nsub),
        grid=(npair, seq // tq),
        in_specs=[
            pl.BlockSpec((batch, tq, LANES), lambda p, i: (0, i, cb + p)),
            pl.BlockSpec((batch, seq, LANES), lambda p, i: (0, 0, cb + npair + p)),
            pl.BlockSpec((batch, seq, LANES), lambda p, i: (0, 0, cb + 2 * npair + p)),
            pl.BlockSpec((2 * blk, 2 * blk), lambda p, i: (0, 0)),
        ],
        out_specs=pl.BlockSpec((batch, tq, LANES), lambda p, i: (0, i, p)),
        out_shape=jax.ShapeDtypeStruct((batch, seq, SB_WIDTH), BRANCH_DTYPE),
        scratch_shapes=[pltpu.VMEM((batch, nsub * 2 * blk, blk), F32), pltpu.VMEM((batch, tq, LANES), F32)],
        compiler_params=_cparams(("parallel", "arbitrary")),
    )(proj3, proj3, proj3, _sb_consts(blk))
    return out.reshape(T, SB_WIDTH)


HG_LEVELS = (32, 16, 8, 4, 2, 1)


def _hg_consts():
    C = HG_CHUNK
    i = np.arange(C)
    sels = [i[None, :] <= i[:, None]]
    masks = []
    for m in HG_LEVELS:
        g = (i // (2 * m)) * (2 * m) + m - 1
        sels.append(i[None, :] <= g[:, None])
        same = (i[:, None] // (2 * m)) == (i[None, :] // (2 * m))
        upper_t = (i[:, None] % (2 * m)) >= m
        lower_s = (i[None, :] % (2 * m)) < m
        masks.append(same & upper_t & lower_s)
    masks.append(np.eye(C, dtype=bool))
    big = np.concatenate(sels, 0).astype(np.float32)
    big3 = np.concatenate([big, big, big], 1)
    return jnp.asarray(big3, dtype=BF16), jnp.asarray(np.stack(masks).astype(np.float32))


def _hg_kernel(f_ref, q_ref, i_ref, g_ref, lb_ref, nw_ref, sel_ref, msk_ref, o_ref, st_ref, *, nchunk):
    C = HG_CHUNK

    @pl.when(pl.program_id(1) == 0)
    def _():
        st_ref[...] = jnp.zeros_like(st_ref)

    sel = sel_ref[...]
    nw = nw_ref[...]
    lb = lb_ref[...]
    nlvl = len(HG_LEVELS)
    for c in range(nchunk):
        rows = slice(c * C, (c + 1) * C)
        sp, sn = _sigmoid_pair(f_ref[rows, :])
        f = lb + (1.0 - lb) * sp
        log_f = jnp.log(jnp.maximum(f, HG_F_FLOOR))
        key = (1.0 - lb) * sn
        q = _silu(q_ref[rows, :]) * (HG_DK ** -0.5)
        val = i_ref[rows, :].astype(BF16)
        br = _dot(sel, _split3(log_f, 0))
        b = br[:C]
        qk = []
        for lvl in range(nlvl):
            r = br[(lvl + 1) * C:(lvl + 2) * C]
            qk.append(((q * jnp.exp(jnp.minimum(b - r, 0.0))).astype(BF16),
                       (key * jnp.exp(jnp.minimum(r - b, 0.0))).astype(BF16)))
        qk.append((q.astype(BF16), key.astype(BF16)))
        q_in = (q * jnp.exp(b)).astype(BF16)
        b_last = b[C - 1:C, :]
        kt = (key * jnp.exp(b_last - b)).astype(BF16)
        decay = jnp.exp(b_last)
        outs = []
        for h in range(HG_HEADS):
            cols = slice(h * HG_DK, (h + 1) * HG_DK)
            s = msk_ref[0] * _dot_nt(qk[0][0][:, cols], qk[0][1][:, cols])
            for lvl in range(1, nlvl + 1):
                s = s + msk_ref[lvl] * _dot_nt(qk[lvl][0][:, cols], qk[lvl][1][:, cols])
            st = st_ref[h]
            o = _dot(s.astype(BF16), val[:, cols]) + _dot_nt(q_in[:, cols], st.astype(BF16))
            st_ref[h] = st * decay[:, cols] + _dot_tn(val[:, cols], kt[:, cols])
            ms = jnp.mean(o * o, axis=-1, keepdims=True)
            outs.append(o * lax.rsqrt(ms + EPS) * nw)
        o_ref[rows, :] = (jnp.concatenate(outs, axis=1) * _silu(g_ref[rows, :])).astype(o_ref.dtype)


def _hgrn2(proj, lb, norm_w, batch, seq, ct):
    T = proj.shape[0]
    W = HG_HEADS * HG_DK
    nc = seq // ct
    sel, msk = _hg_consts()
    cb = COL_HG // W
    spec = lambda k: pl.BlockSpec((ct, W), lambda b, c: (b * nc + c, cb + k))
    return pl.pallas_call(
        functools.partial(_hg_kernel, nchunk=ct // HG_CHUNK),
        grid=(batch, nc),
        in_specs=[spec(0), spec(1), spec(2), spec(3),
                  pl.BlockSpec((1, W), lambda b, c: (0, 0)),
                  pl.BlockSpec((1, HG_DK), lambda b, c: (0, 0)),
                  pl.BlockSpec(sel.shape, lambda b, c: (0, 0)),
                  pl.BlockSpec(msk.shape, lambda b, c: (0, 0, 0))],
        out_specs=pl.BlockSpec((ct, W), lambda b, c: (b * nc + c, 0)),
        out_shape=jax.ShapeDtypeStruct((T, W), BRANCH_DTYPE),
        scratch_shapes=[pltpu.VMEM((HG_HEADS, HG_DK, HG_DK), F32)],
        compiler_params=_cparams(("parallel", "arbitrary")),
    )(proj, proj, proj, proj, lb.reshape(1, W), norm_w.reshape(1, HG_DK), sel, msk)


SSD_PAD = 8


def _ssd_kernel(z_ref, xbc_ref, dt_ref, cw_ref, cb_ref, dtb_ref, a_ref, d_ref, nw_ref, tril3_ref,
                o_ref, pad_ref, st_ref):
    L = SSD_CHUNK
    W = SSD_WIDTH
    GN = 2 * SSD_STATE

    @pl.when(pl.program_id(1) == 0)
    def _():
        pad_ref[0:SSD_PAD, :] = jnp.zeros((SSD_PAD, SSD_CONV_DIM), F32)
        st_ref[...] = jnp.zeros_like(st_ref)

    pad_ref[SSD_PAD:SSD_PAD + L, :] = xbc_ref[...]
    conv = cb_ref[...]
    for k in range(SSD_CONV):
        off = SSD_PAD - (SSD_CONV - 1) + k
        conv = conv + cw_ref[k:k + 1, :] * pad_ref[off:off + L, :]
    pad_ref[0:SSD_PAD, :] = pad_ref[L:L + SSD_PAD, :]
    xa = _silu(conv)
    xs = xa[:, :W]
    bm = xa[:, W:W + GN]
    cm = xa[:, W + GN:W + 2 * GN]
    dtv = dt_ref[...] + dtb_ref[...]
    dt = jnp.maximum(dtv, 0.0) + jnp.log1p(jnp.exp(-jnp.abs(dtv)))
    a = a_ref[...] * dt
    tril3 = tril3_ref[...]
    a_cum = _dot(tril3, _split3(a, 0))
    a_last = a_cum[L - 1:L, :]
    xdt = xs * dt

    lane = lax.broadcasted_iota(jnp.int32, (1, LANES), 1)
    half = [lane < SSD_STATE, lane >= SSD_STATE]
    row = lax.broadcasted_iota(jnp.int32, (L, L), 0)
    col = lax.broadcasted_iota(jnp.int32, (L, L), 1)
    strict = col < row
    causal = col <= row
    bb = bm.astype(BF16)
    cbm = cm.astype(BF16)
    cb_g = [_dot_nt(jnp.where(half[g], cm, 0.0).astype(BF16), bb) for g in range(2)]

    y_parts = []
    for pair in range(SSD_HEADS // 2):
        g = pair // 2
        xp = xdt[:, pair * LANES:(pair + 1) * LANES]
        acc = jnp.zeros((L, LANES), F32)
        for hh in range(2):
            h = 2 * pair + hh
            a_col = a[:, h * SSD_HEAD_DIM:h * SSD_HEAD_DIM + 1]
            diff = _dot(tril3, _split3(jnp.where(strict, a_col, 0.0), 0))
            seg = jnp.exp(jnp.where(causal, diff, NEG_BIG))
            sc = (cb_g[g] * seg).astype(BF16)
            acc = acc + _dot(sc, jnp.where(half[hh], xp, 0.0).astype(BF16))
        y_parts.append(acc)
    y = jnp.concatenate(y_parts, axis=1)

    st = st_ref[...]
    y = y + jnp.exp(a_cum) * _dot(cbm, st.astype(BF16))
    srow = lax.broadcasted_iota(jnp.int32, (GN, W), 0) // SSD_STATE
    scol = lax.broadcasted_iota(jnp.int32, (GN, W), 1) // (W // 2)
    upd = _dot_tn(bb, (jnp.exp(a_last - a_cum) * xdt).astype(BF16))
    st_ref[...] = st * jnp.exp(a_last) + jnp.where(srow == scol, upd, 0.0)

    y = y + d_ref[...] * xs
    y = y * _silu(z_ref[...])
    nw = nw_ref[...]
    outs = []
    for g in range(2):
        yg = y[:, g * (W // 2):(g + 1) * (W // 2)]
        ms = jnp.mean(yg * yg, axis=-1, keepdims=True)
        outs.append(yg * lax.rsqrt(ms + EPS) * nw[:, g * (W // 2):(g + 1) * (W // 2)])
    o_ref[...] = jnp.concatenate(outs, axis=1).astype(o_ref.dtype)


def _ssd(proj, conv_w, conv_b, dt_bias, a_log, d_skip, norm_w, batch, seq):
    T = proj.shape[0]
    L = SSD_CHUNK
    W = SSD_WIDTH
    nc = seq // L
    rep = lambda v: jnp.repeat(v.astype(F32), SSD_HEAD_DIM).reshape(1, W)
    tril = np.tril(np.ones((L, L), np.float32))
    tril3 = jnp.asarray(np.concatenate([tril, tril, tril], 1), dtype=BF16)
    vec = lambda n: pl.BlockSpec((1, n), lambda b, c: (0, 0))
    return pl.pallas_call(
        _ssd_kernel,
        grid=(batch, nc),
        in_specs=[
            pl.BlockSpec((L, W), lambda b, c: (b * nc + c, COL_Z // W)),
            pl.BlockSpec((L, SSD_CONV_DIM), lambda b, c: (b * nc + c, COL_XBC // SSD_CONV_DIM)),
            pl.BlockSpec((L, W), lambda b, c: (b * nc + c, COL_DT // W)),
            pl.BlockSpec((SSD_CONV, SSD_CONV_DIM), lambda b, c: (0, 0)),
            vec(SSD_CONV_DIM), vec(W), vec(W), vec(W), vec(W),
            pl.BlockSpec((L, 3 * L), lambda b, c: (0, 0)),
        ],
        out_specs=pl.BlockSpec((L, W), lambda b, c: (b * nc + c, 0)),
        out_shape=jax.ShapeDtypeStruct((T, W), BRANCH_DTYPE),
        scratch_shapes=[pltpu.VMEM((L + SSD_PAD, SSD_CONV_DIM), F32),
                        pltpu.VMEM((2 * SSD_STATE, W), F32)],
        compiler_params=_cparams(("parallel", "arbitrary")),
    )(proj, proj, proj, conv_w, conv_b.reshape(1, SSD_CONV_DIM), rep(dt_bias),
      rep(-jnp.exp(a_log.astype(F32))), rep(d_skip), norm_w.reshape(1, W), tril3)


def _merge_kernel(ysb_ref, yhg_ref, yssd_ref, g0_ref, g1_ref, g2_ref, x_ref,
                  w0_ref, w1_ref, w2_ref, wo_ref, nw_ref, xo_ref, h_ref):
    m = _sigmoid_pair(g0_ref[...])[0] * _dot(ysb_ref[...], w0_ref[...])
    m = m + _sigmoid_pair(g1_ref[...])[0] * _dot(yhg_ref[...], w1_ref[...])
    m = m + _sigmoid_pair(g2_ref[...])[0] * _dot(yssd_ref[...], w2_ref[...])
    xn = x_ref[...] + _dot(m.astype(BF16), wo_ref[...])
    xo_ref[...] = xn
    ms = jnp.mean(xn * xn, axis=-1, keepdims=True)
    h_ref[...] = (xn * lax.rsqrt(ms + EPS) * nw_ref[...]).astype(BF16)


def _merge(y_sb, y_hg, y_ssd, proj, x, w_sb, w_hg, w_ssd, w_out, norm_w, tm):
    T, D = x.shape
    Wb = y_sb.shape[1]
    yspec = pl.BlockSpec((tm, Wb), lambda i: (i, 0))
    gspec = lambda k: pl.BlockSpec((tm, D), lambda i: (i, COL_GATE // D + k))
    wspec = pl.BlockSpec((Wb, D), lambda i: (0, 0))
    xspec = pl.BlockSpec((tm, D), lambda i: (i, 0))
    return pl.pallas_call(
        _merge_kernel,
        grid=(T // tm,),
        in_specs=[yspec, yspec, yspec, gspec(0), gspec(1), gspec(2), xspec,
                  wspec, wspec, wspec,
                  pl.BlockSpec((D, D), lambda i: (0, 0)),
                  pl.BlockSpec((1, D), lambda i: (0, 0))],
        out_specs=[xspec, xspec],
        out_shape=[jax.ShapeDtypeStruct((T, D), F32), jax.ShapeDtypeStruct((T, D), BF16)],
        compiler_params=_cparams(("parallel",)),
    )(y_sb, y_hg, y_ssd, proj, proj, proj, x, w_sb, w_hg, w_ssd, w_out, norm_w.reshape(1, D))


def _peerq_kernel(h_ref, wq_ref, sk_ref, s1_ref, s2_ref, ht_ref):
    h = h_ref[...]
    q = _dot(h, wq_ref[...]).astype(BF16)
    for hd in range(PEER_HEADS):
        for p, s_ref in enumerate((s1_ref, s2_ref)):
            j = hd * 2 + p
            s_ref[hd] = _dot_nt(sk_ref[j], q[:, j * PEER_NKEYS:(j + 1) * PEER_NKEYS])
    ht_ref[...] = h.astype(F32).T.astype(BF16)


def _peer_query(h2, w_q, sub_keys, tm):
    T, D = h2.shape
    NK = PEER_NKEYS
    sspec = pl.BlockSpec((PEER_HEADS, NK, tm), lambda i: (0, 0, i))
    return pl.pallas_call(
        _peerq_kernel,
        grid=(T // tm,),
        in_specs=[pl.BlockSpec((tm, D), lambda i: (i, 0)),
                  pl.BlockSpec(w_q.shape, lambda i: (0, 0)),
                  pl.BlockSpec(sub_keys.shape, lambda i: (0, 0, 0))],
        out_specs=[sspec, sspec, pl.BlockSpec((None, D, tm), lambda i: (i, 0, 0))],
        out_shape=[jax.ShapeDtypeStruct((PEER_HEADS, NK, T), F32),
                   jax.ShapeDtypeStruct((PEER_HEADS, NK, T), F32),
                   jax.ShapeDtypeStruct((T // tm, D, tm), BF16)],
        compiler_params=_cparams(("parallel",)),
    )(h2, w_q, sub_keys)


def _oddeven_merge(lo, hi, r):
    step = r * 2
    if step < hi - lo:
        yield from _oddeven_merge(lo, hi, step)
        yield from _oddeven_merge(lo + r, hi, step)
        yield from [(i, i + r) for i in range(lo + r, hi - r, step)]
    else:
        yield (lo, lo + r)


def _oddeven_sort(lo, hi):
    if hi - lo >= 1:
        mid = lo + (hi - lo) // 2
        yield from _oddeven_sort(lo, mid)
        yield from _oddeven_sort(mid + 1, hi)
        yield from _oddeven_merge(lo, hi, 1)


SORT16 = tuple(_oddeven_sort(0, PEER_TOPK - 1))


def _cmp_exchange(x, i, j):
    x[i], x[j] = jnp.maximum(x[i], x[j]), jnp.minimum(x[i], x[j])


def _top16(x, n_valid):
    K = PEER_TOPK
    for i, j in SORT16:
        if j < n_valid:
            _cmp_exchange(x, i, j)
    for shift in (4, 2, 1):
        y = [pltpu.roll(v, shift, 0) for v in x]
        x = [jnp.maximum(x[i], y[K - 1 - i]) for i in range(K)]
        d = K // 2
        while d >= 1:
            for i in range(K):
                if i & d == 0:
                    _cmp_exchange(x, i, i + d)
            d //= 2
    return x


def _count_above(v, x):
    assert len(v) == 16
    c8 = v[7] > x
    c4 = jnp.where(c8, v[11], v[3]) > x
    c2 = jnp.where(c8, jnp.where(c4, v[13], v[9]), jnp.where(c4, v[5], v[1])) > x
    hi = jnp.where(c4, jnp.where(c2, v[14], v[12]), jnp.where(c2, v[10], v[8]))
    lo = jnp.where(c4, jnp.where(c2, v[6], v[4]), jnp.where(c2, v[2], v[0]))
    c1 = jnp.where(c8, hi, lo) > x
    count = jnp.where(c8, 8.0, 0.0) + jnp.where(c4, 4.0, 0.0) + jnp.where(c2, 2.0, 0.0)
    return count + jnp.where(c1, 1.0, 0.0) + jnp.where(v[15] > x, 1.0, 0.0)


def _pack_rows(rep, sub):
    out = rep[0]
    for i in range(1, SUBLANES):
        out = jnp.where(sub == i, rep[i], out)
    return out


def _peer_topk_kernel(s1_ref, s2_ref, rank_ref, e2_ref, n_ref, r1_ref):
    for hd in range(s1_ref.shape[0]):
        _peer_topk_head(hd, s1_ref, s2_ref, rank_ref, e2_ref, n_ref, r1_ref)


def _peer_topk_head(hd, s1_ref, s2_ref, rank_ref, e2_ref, n_ref, r1_ref):
    K = PEER_TOPK
    G = PEER_NKEYS // SUBLANES
    tt = s1_ref.shape[2]
    s1 = [s1_ref[hd, g * SUBLANES:(g + 1) * SUBLANES, :] for g in range(G)]
    s2 = [s2_ref[hd, g * SUBLANES:(g + 1) * SUBLANES, :] for g in range(G)]
    v1 = _top16(list(s1), G)
    v2 = _top16(list(s2), G)
    sub = lax.broadcasted_iota(jnp.int32, (SUBLANES, tt), 0)
    v1p = [_pack_rows(v1[:SUBLANES], sub), _pack_rows(v1[SUBLANES:], sub)]
    v2p_hi = _pack_rows(v2[SUBLANES:], sub)
    cand = [v1p[0] + v2[0], v1p[1] + v2[0]] + [v1p[0] + v2[j] for j in range(1, SUBLANES)]
    cand.append(v1[0] + v2p_hi)
    ninf = jnp.full((SUBLANES, tt), -jnp.inf, F32)
    top = _top16(cand + [ninf] * (K - len(cand)), len(cand))
    tau = top[K - 1]
    zsum = jnp.ones_like(tau)
    for r in range(1, K):
        zsum = zsum + jnp.exp(top[r] - top[0])
    cnt = [jnp.zeros((SUBLANES, tt), F32), jnp.zeros((SUBLANES, tt), F32)]
    for j in range(K):
        for k in range(2):
            cnt[k] = cnt[k] + jnp.where(v1p[k] + v2[j] >= tau, 1.0, 0.0)
    n_rep = [jnp.broadcast_to(cnt[i // SUBLANES][i % SUBLANES:i % SUBLANES + 1, :], (SUBLANES, tt))
             for i in range(K)]
    n_rep = [n_rep[0]] + [jnp.where(v1[i] < v1[i - 1], n_rep[i], 0.0) for i in range(1, K)]
    inv_z = 1.0 / zsum
    for g in range(G):
        rows = slice(g * SUBLANES, (g + 1) * SUBLANES)
        n = jnp.zeros((SUBLANES, tt), F32)
        for i in range(K - 1, -1, -1):
            n = jnp.where(s1[g] == v1[i], n_rep[i], n)
        n_ref[hd, rows, :] = n
        r1_ref[hd, rows, :] = jnp.exp(s1[g] - v1[0]) * inv_z
    for g in range(0, G, 2):
        rk, e2 = [], []
        for gg in (g, g + 1):
            rk.append(_count_above(v2, s2[gg]))
            e2.append(jnp.exp(s2[gg] - v2[0]))
        rows = slice(g * SUBLANES, (g + 2) * SUBLANES)
        rank_ref[hd, rows, :] = jnp.concatenate(rk, axis=0).astype(BF16)
        e2_ref[hd, rows, :] = jnp.concatenate(e2, axis=0).astype(BF16)


def _peer_topk(s1t, s2t, tt, hb):
    H, NK, T = s1t.shape
    sspec = pl.BlockSpec((hb, NK, tt), lambda i, h: (h, 0, i))
    return pl.pallas_call(
        _peer_topk_kernel,
        grid=(T // tt, H // hb),
        in_specs=[sspec, sspec],
        out_specs=[sspec, sspec, sspec, sspec],
        out_shape=[jax.ShapeDtypeStruct((H, NK, T), BF16), jax.ShapeDtypeStruct((H, NK, T), BF16),
                   jax.ShapeDtypeStruct((H, NK, T), F32), jax.ShapeDtypeStruct((H, NK, T), F32)],
        compiler_params=_cparams(("parallel", "parallel")),
    )(s1t, s2t)


def _peer_kernel(x_ref, ht_ref, htn_ref, rank_ref, e2_ref, n_ref, r1_ref, u0_ref, u1_ref, ua_ref, ub_ref,
                 va_ref, vb_ref, fw_ref, o_ref, act_a, act_b, p_a, p_b, acc_ref, *, eb, tt, final):
    s = pl.program_id(1)
    NK = PEER_NKEYS
    assert eb // NK == SUBLANES
    zero = jnp.zeros((), BF16)

    def build_p(blk, act_ref, p_ref, tc):
        a0 = pl.multiple_of(blk * SUBLANES, SUBLANES)
        ts = slice(tc * LANES, (tc + 1) * LANES)
        n8 = [n_ref[h, pl.ds(a0, SUBLANES), ts] for h in range(PEER_HEADS)]
        r8 = [r1_ref[h, pl.ds(a0, SUBLANES), ts] for h in range(PEER_HEADS)]
        for ai in range(SUBLANES):
            w = None
            for h in range(PEER_HEADS):
                n_row = n8[h][ai:ai + 1, :].astype(BF16)
                r_row = r8[h][ai:ai + 1, :].astype(BF16)
                term = jnp.where(rank_ref[h, :, ts] < n_row, e2_ref[h, :, ts] * r_row, zero)
                w = term if w is None else w + term
            act = act_ref[ai * NK:(ai + 1) * NK, ts]
            gelu = 0.5 * act * (1.0 + lax.erf(act * (2.0 ** -0.5)))
            p_ref[ai * NK:(ai + 1) * NK, ts] = w * gelu.astype(BF16)

    @pl.when(s == 0)
    def _():
        acc_ref[...] = jnp.zeros_like(acc_ref)

    @pl.when((s == 0) & (pl.program_id(0) == 0))
    def _():
        act_a[...] = _dot(u0_ref[...], ht_ref[...])
        act_b[...] = _dot(u1_ref[...], ht_ref[...])

    half = tt // 2
    per_half = half // LANES

    def stage(blk, act_ref, p_ref, v_ref, u_next_ref):
        for c in range(2):
            cs = slice(c * half, (c + 1) * half)
            for tc in range(c * per_half, (c + 1) * per_half):
                build_p(blk, act_ref, p_ref, tc)
            acc_ref[:, cs] += _dot(v_ref[...], p_ref[:, cs])
            act_ref[:, cs] = _dot(u_next_ref[...], htn_ref[:, cs])

    stage(2 * s, act_a, p_a, va_ref, ua_ref)
    stage(2 * s + 1, act_b, p_b, vb_ref, ub_ref)

    @pl.when(s == pl.num_programs(1) - 1)
    def _():
        y = x_ref[...] + acc_ref[...].T
        if final:
            ms = jnp.mean(y * y, axis=-1, keepdims=True)
            y = y * lax.rsqrt(ms + EPS) * fw_ref[...]
        o_ref[...] = y


def _peer(x, h2t, rank2, e2, n1, r1, u, vt, final_w, tt, eb, final):
    T, D = x.shape
    H, NK = PEER_HEADS, PEER_NKEYS
    ne = vt.shape[0]
    assert h2t.shape == (T // tt, D, tt) and vt.shape == (ne, D, eb) and u.shape == (ne * eb, D)
    sspec = pl.BlockSpec((H, NK, tt), lambda i, s: (0, 0, i))
    once = pl.Buffered(1)
    nt, ns = T // tt, ne // 2
    return pl.pallas_call(
        functools.partial(_peer_kernel, eb=eb, tt=tt, final=final),
        grid=(nt, ns),
        in_specs=[pl.BlockSpec((tt, D), lambda i, s: (i, 0)),
                  pl.BlockSpec((None, D, tt), lambda i, s: (i, 0, 0)),
                  pl.BlockSpec((None, D, tt), lambda i, s: (jnp.minimum(i + (s + 1) // ns, nt - 1), 0, 0)),
                  sspec, sspec, sspec, sspec,
                  pl.BlockSpec((eb, D), lambda i, s: (0, 0), pipeline_mode=once),
                  pl.BlockSpec((eb, D), lambda i, s: (1, 0), pipeline_mode=once),
                  pl.BlockSpec((eb, D), lambda i, s: ((2 * s + 2) % ne, 0)),
                  pl.BlockSpec((eb, D), lambda i, s: ((2 * s + 3) % ne, 0)),
                  pl.BlockSpec((None, D, eb), lambda i, s: (2 * s, 0, 0)),
                  pl.BlockSpec((None, D, eb), lambda i, s: (2 * s + 1, 0, 0)),
                  pl.BlockSpec((1, D), lambda i, s: (0, 0))],
        out_specs=pl.BlockSpec((tt, D), lambda i, s: (i, 0)),
        out_shape=jax.ShapeDtypeStruct((T, D), F32),
        scratch_shapes=[pltpu.VMEM((eb, tt), F32), pltpu.VMEM((eb, tt), F32),
                        pltpu.VMEM((eb, tt), BF16), pltpu.VMEM((eb, tt), BF16),
                        pltpu.VMEM((D, tt), F32)],
        compiler_params=_cparams(("arbitrary", "arbitrary")),
    )(x, h2t, h2t, rank2, e2, n1, r1, u, u, u, u, vt, vt, final_w.reshape(1, D))


def _layer(x, batch, seq, lb, p, final_w, final, tiles):
    proj = _inproj(x, p["norm1_w"], p["w_in"], tiles["in_tm"], tiles["in_tn"])
    y_sb = _sb_attention(proj, batch, seq, tiles["sb_blk"], tiles["sb_nsub"])
    y_hg = _hgrn2(proj, lb, p["hg_norm_w"], batch, seq, tiles["hg_ct"])
    y_ssd = _ssd(proj, p["ssd_conv_w"], p["ssd_conv_b"], p["ssd_dt_bias"], p["ssd_a_log"],
                 p["ssd_d"], p["ssd_norm_w"], batch, seq)
    x, h2 = _merge(y_sb, y_hg, y_ssd, proj, x, p["w_branch_sb"], p["w_branch_hg"],
                   p["w_branch_ssd"], p["w_out"], p["norm2_w"], tiles["mg_tm"])
    s1t, s2t, h2t = _peer_query(h2, p["peer_w_q"], p["peer_sub_keys"], tiles["pq_tm"])
    rank2, e2, n1, r1 = _peer_topk(s1t, s2t, tiles["tk_tt"], tiles["tk_hb"])
    return _peer(x, h2t, rank2, e2, n1, r1, p["peer_u"], p["peer_vt"], final_w,
                 tiles["pe_tt"], tiles["pe_eb"], final)


TILES = dict(in_tm=2048, in_tn=768, sb_blk=128, sb_nsub=2, hg_ct=256, mg_tm=256, pq_tm=512,
             tk_tt=256, tk_hb=2, pe_tt=512, pe_eb=1024)


def _prep_w_in(w):
    c = np.cumsum([0, 512, 512, 512, 512, 512, 512, 512, 512, 768, 8, 1024, 1024, 1024])
    dt_rep = jnp.repeat(w[:, c[9]:c[10]], SSD_HEAD_DIM, axis=1)
    return jnp.concatenate([w[:, c[3]:c[7]], w[:, c[7]:c[8]], dt_rep, w[:, c[10]:c[13]],
                            w[:, c[8]:c[9]], w[:, c[0]:c[3]]], axis=1).astype(BF16)


def _trunk(x, norm1_w, w_in, hg_lb_logits, hg_norm_w, ssd_conv_w, ssd_conv_b, ssd_dt_bias,
           ssd_a_log, ssd_d, ssd_norm_w, w_branch_sb, w_branch_hg, w_branch_ssd, w_out,
           norm2_w, peer_w_q, peer_sub_keys, peer_u, peer_v, final_norm_w, tiles):
    batch, seq, d = x.shape
    depth = w_in.shape[0]
    gamma = jax.nn.softmax(hg_lb_logits.astype(F32), axis=0)
    lower_bounds = jnp.cumsum(gamma, axis=0) - gamma[0]
    xt = x.reshape(batch * seq, d)
    for l in range(depth):
        p = dict(
            norm1_w=norm1_w[l], w_in=_prep_w_in(w_in[l]), hg_norm_w=hg_norm_w[l],
            ssd_conv_w=ssd_conv_w[l], ssd_conv_b=ssd_conv_b[l], ssd_dt_bias=ssd_dt_bias[l],
            ssd_a_log=ssd_a_log[l], ssd_d=ssd_d[l], ssd_norm_w=ssd_norm_w[l],
            w_branch_sb=w_branch_sb[l].astype(BF16), w_branch_hg=w_branch_hg[l].astype(BF16),
            w_branch_ssd=w_branch_ssd[l].astype(BF16), w_out=w_out[l].astype(BF16),
            norm2_w=norm2_w[l], peer_w_q=peer_w_q[l].astype(BF16),
            peer_sub_keys=peer_sub_keys[l].reshape(2 * PEER_HEADS, PEER_NKEYS, -1).astype(BF16),
            peer_u=peer_u[l].astype(BF16),
            peer_vt=peer_v[l].reshape(-1, tiles["pe_eb"], d).transpose(0, 2, 1).astype(BF16))
        xt = _layer(xt, batch, seq, lower_bounds[l], p, final_norm_w, l == depth - 1, tiles)
    return xt.reshape(batch, seq, d)


def kernel(x, norm1_w, w_in, hg_lb_logits, hg_norm_w, ssd_conv_w, ssd_conv_b, ssd_dt_bias, ssd_a_log, ssd_d, ssd_norm_w, w_branch_sb, w_branch_hg, w_branch_ssd, w_out, norm2_w, peer_w_q, peer_sub_keys, peer_u, peer_v, final_norm_w):
    return _trunk(x, norm1_w, w_in, hg_lb_logits, hg_norm_w, ssd_conv_w, ssd_conv_b, ssd_dt_bias,
                  ssd_a_log, ssd_d, ssd_norm_w, w_branch_sb, w_branch_hg, w_branch_ssd, w_out,
                  norm2_w, peer_w_q, peer_sub_keys, peer_u, peer_v, final_norm_w, TILES)
```

```python
import functools

import numpy as np
import jax
import jax.numpy as jnp
from jax import lax
from jax.experimental import pallas as pl
from jax.experimental.pallas import tpu as pltpu

F32 = jnp.float32
BF16 = jnp.bfloat16
BRANCH_DTYPE = BF16

D_MODEL = 1024
SB_HEAD_DIM = 64
SB_WIDTH = 512
HG_HEADS = 4
HG_DK = 128
HG_CHUNK = 64
HG_F_FLOOR = 1e-30
SSD_HEADS = 8
SSD_HEAD_DIM = 64
SSD_WIDTH = 512
SSD_STATE = 64
SSD_CONV = 4
SSD_CHUNK = 128
SSD_CONV_DIM = 768
PEER_HEADS = 8
PEER_NKEYS = 128
PEER_TOPK = 16
EPS = 1e-6
NEG_BIG = -1e30

LANES = 128
SUBLANES = 8
VMEM_LIMIT = 56 * 1024 * 1024

COL_HG = 0
COL_Z = 2048
COL_DT = 2560
COL_GATE = 3072
COL_XBC = 6144
COL_SB = 6912
IN_COLS_PADDED = 8448


def _cparams(sem):
    return pltpu.CompilerParams(dimension_semantics=sem, vmem_limit_bytes=VMEM_LIMIT)


def _dot(a, b):
    return jnp.dot(a, b, preferred_element_type=F32)


def _dot_nt(a, b):
    return lax.dot_general(a, b, (((1,), (1,)), ((), ())), preferred_element_type=F32)


def _dot_tn(a, b):
    return lax.dot_general(a, b, (((0,), (0,)), ((), ())), preferred_element_type=F32)


def _sigmoid_pair(x):
    e = jnp.exp(-jnp.abs(x))
    inv = 1.0 / (1.0 + e)
    small = e * inv
    pos = x >= 0
    return jnp.where(pos, inv, small), jnp.where(pos, small, inv)


def _silu(x):
    return x * _sigmoid_pair(x)[0]


def _split3(x, axis):
    hi = x.astype(BF16)
    r1 = x - hi.astype(F32)
    mid = r1.astype(BF16)
    lo = (r1 - mid.astype(F32)).astype(BF16)
    return jnp.concatenate([hi, mid, lo], axis=axis)


def _inproj_kernel(x_ref, nw_ref, w_ref, o_ref, h_ref):
    @pl.when(pl.program_id(1) == 0)
    def _():
        x = x_ref[...]
        ms = jnp.mean(x * x, axis=-1, keepdims=True)
        h_ref[...] = (x * lax.rsqrt(ms + EPS) * nw_ref[...]).astype(BF16)

    o_ref[...] = _dot(h_ref[...], w_ref[...])


def _inproj(x, norm_w, w, tm, tn):
    T, D = x.shape
    N = w.shape[1]
    return pl.pallas_call(
        _inproj_kernel,
        grid=(T // tm, N // tn),
        in_specs=[
            pl.BlockSpec((tm, D), lambda i, j: (i, 0)),
            pl.BlockSpec((1, D), lambda i, j: (0, 0)),
            pl.BlockSpec((D, tn), lambda i, j: (0, j)),
        ],
        out_specs=pl.BlockSpec((tm, tn), lambda i, j: (i, j)),
        out_shape=jax.ShapeDtypeStruct((T, N), F32),
        scratch_shapes=[pltpu.VMEM((tm, D), BF16)],
        compiler_params=_cparams(("parallel", "arbitrary")),
    )(x, norm_w.reshape(1, D), w)


SB_EXIT = -110.0


def _sb_kernel(q_ref, k_ref, v_ref, mcat_ref, o_ref, c_ref, acc_ref, *, blk, nsub):
    nseq = q_ref.shape[0]
    qi = pl.program_id(1)
    lane = lax.broadcasted_iota(jnp.int32, (1, LANES), 1)
    first_head = lane < SB_HEAD_DIM
    mcat = mcat_ref[...]
    qs = []
    for g in range(nseq):
        qparts = []
        for s in range(nsub):
            q = q_ref[g, s * blk:(s + 1) * blk, :] * (SB_HEAD_DIM ** -0.5)
            qparts += [jnp.where(first_head, q, 0.0), jnp.where(first_head, 0.0, q)]
        qs.append(jnp.concatenate(qparts, axis=0).astype(BF16))
    c_ref[...] = jnp.zeros_like(c_ref)
    acc_ref[...] = jnp.zeros_like(acc_ref)

    def visit(kb, s0, diag):
        r0 = s0 * 2 * blk
        n = (nsub - s0) * 2 * blk
        ks = pl.multiple_of(kb * blk, blk)
        z = jnp.concatenate([_dot_nt(qs[g][r0:], k_ref[g, pl.ds(ks, blk), :].astype(BF16))
                             for g in range(nseq)], axis=0)
        t = jnp.log1p(jnp.exp(-jnp.abs(z)))
        log_beta = jnp.minimum(z, 0.0) - t
        log_rest = jnp.minimum(-z, 0.0) - t
        if diag:
            row = lax.rem(lax.broadcasted_iota(jnp.int32, z.shape, 0), n)
            col = lax.broadcasted_iota(jnp.int32, z.shape, 1)
            before = (row >= 2 * blk) | (col < (row & (blk - 1)))
            log_rest = jnp.where(before, log_rest, 0.0)
        hi = log_rest.astype(BF16)
        lo = (log_rest - hi.astype(F32)).astype(BF16)
        cs = _dot(jnp.concatenate([hi, lo], axis=1), mcat)
        carry = jnp.concatenate([c_ref[g, r0:, :] for g in range(nseq)], axis=0)
        a = jnp.exp(log_beta + cs[:, :blk] + carry)
        if diag:
            a = jnp.where(before, a, 0.0)
        carry = carry + cs[:, blk:]
        a = a.astype(BF16)
        for g in range(nseq):
            c_ref[g, r0:, :] = carry[g * n:(g + 1) * n]
            av = _dot(a[g * n:(g + 1) * n], v_ref[g, pl.ds(ks, blk), :].astype(BF16))
            for s in range(s0, nsub):
                i = (s - s0) * 2 * blk
                acc_ref[g, s * blk:(s + 1) * blk, :] += jnp.where(first_head, av[i:i + blk], av[i + blk:i + 2 * blk])
        return carry

    for j in range(nsub - 1, -1, -1):
        carry = visit(qi * nsub + j, j, True)

    def body(state):
        kb, _ = state
        return kb - 1, jnp.max(visit(kb, 0, False))

    lax.while_loop(lambda st: (st[0] >= 0) & (st[1] > SB_EXIT), body,
                   (qi * nsub - 1, jnp.max(carry)))
    o_ref[...] = acc_ref[...].astype(o_ref.dtype)


def _sb_consts(blk):
    r = np.arange(2 * blk)[:, None] % blk
    c = np.arange(2 * blk)[None, :]
    return jnp.asarray(((c >= blk) | (r > c)).astype(np.float32), dtype=BF16)


def _sb_attention(proj, batch, seq, blk, nsub):
    T, ncol = proj.shape
    tq = blk * nsub
    cb = COL_SB // LANES
    npair = SB_WIDTH // LANES
    proj3 = proj.reshape(batch, seq, ncol)
    out = pl.pallas_call(
        functools.partial(_sb_kernel, blk=blk, nsub=nsub),
        grid=(npair, seq // tq),
        in_specs=[
            pl.BlockSpec((batch, tq, LANES), lambda p, i: (0, i, cb + p)),
            pl.BlockSpec((batch, seq, LANES), lambda p, i: (0, 0, cb + npair + p)),
            pl.BlockSpec((batch, seq, LANES), lambda p, i: (0, 0, cb + 2 * npair + p)),
            pl.BlockSpec((2 * blk, 2 * blk), lambda p, i: (0, 0)),
        ],
        out_specs=pl.BlockSpec((batch, tq, LANES), lambda p, i: (0, i, p)),
        out_shape=jax.ShapeDtypeStruct((batch, seq, SB_WIDTH), BRANCH_DTYPE),
        scratch_shapes=[pltpu.VMEM((batch, nsub * 2 * blk, blk), F32), pltpu.VMEM((batch, tq, LANES), F32)],
        compiler_params=_cparams(("parallel", "arbitrary")),
    )(proj3, proj3, proj3, _sb_consts(blk))
    return out.reshape(T, SB_WIDTH)


HG_LEVELS = (32, 16, 8, 4, 2, 1)


def _hg_consts():
    C = HG_CHUNK
    i = np.arange(C)
    sels = [i[None, :] <= i[:, None]]
    masks = []
    for m in HG_LEVELS:
        g = (i // (2 * m)) * (2 * m) + m - 1
        sels.append(i[None, :] <= g[:, None])
        same = (i[:, None] // (2 * m)) == (i[None, :] // (2 * m))
        upper_t = (i[:, None] % (2 * m)) >= m
        lower_s = (i[None, :] % (2 * m)) < m
        masks.append(same & upper_t & lower_s)
    masks.append(np.eye(C, dtype=bool))
    big = np.concatenate(sels, 0).astype(np.float32)
    big3 = np.concatenate([big, big, big], 1)
    return jnp.asarray(big3, dtype=BF16), jnp.asarray(np.stack(masks).astype(np.float32))


def _hg_kernel(f_ref, q_ref, i_ref, g_ref, lb_ref, nw_ref, sel_ref, msk_ref, o_ref, st_ref, *, nchunk):
    C = HG_CHUNK

    @pl.when(pl.program_id(1) == 0)
    def _():
        st_ref[...] = jnp.zeros_like(st_ref)

    sel = sel_ref[...]
    nw = nw_ref[...]
    lb = lb_ref[...]
    nlvl = len(HG_LEVELS)
    for c in range(nchunk):
        rows = slice(c * C, (c + 1) * C)
        sp, sn = _sigmoid_pair(f_ref[rows, :])
        f = lb + (1.0 - lb) * sp
        log_f = jnp.log(jnp.maximum(f, HG_F_FLOOR))
        key = (1.0 - lb) * sn
        q = _silu(q_ref[rows, :]) * (HG_DK ** -0.5)
        val = i_ref[rows, :].astype(BF16)
        br = _dot(sel, _split3(log_f, 0))
        b = br[:C]
        qk = []
        for lvl in range(nlvl):
            r = br[(lvl + 1) * C:(lvl + 2) * C]
            qk.append(((q * jnp.exp(jnp.minimum(b - r, 0.0))).astype(BF16),
                       (key * jnp.exp(jnp.minimum(r - b, 0.0))).astype(BF16)))
        qk.append((q.astype(BF16), key.astype(BF16)))
        q_in = (q * jnp.exp(b)).astype(BF16)
        b_last = b[C - 1:C, :]
        kt = (key * jnp.exp(b_last - b)).astype(BF16)
        decay = jnp.exp(b_last)
        outs = []
        for h in range(HG_HEADS):
            cols = slice(h * HG_DK, (h + 1) * HG_DK)
            s = msk_ref[0] * _dot_nt(qk[0][0][:, cols], qk[0][1][:, cols])
            for lvl in range(1, nlvl + 1):
                s = s + msk_ref[lvl] * _dot_nt(qk[lvl][0][:, cols], qk[lvl][1][:, cols])
            st = st_ref[h]
            o = _dot(s.astype(BF16), val[:, cols]) + _dot_nt(q_in[:, cols], st.astype(BF16))
            st_ref[h] = st * decay[:, cols] + _dot_tn(val[:, cols], kt[:, cols])
            ms = jnp.mean(o * o, axis=-1, keepdims=True)
            outs.append(o * lax.rsqrt(ms + EPS) * nw)
        o_ref[rows, :] = (jnp.concatenate(outs, axis=1) * _silu(g_ref[rows, :])).astype(o_ref.dtype)


def _hgrn2(proj, lb, norm_w, batch, seq, ct):
    T = proj.shape[0]
    W = HG_HEADS * HG_DK
    nc = seq // ct
    sel, msk = _hg_consts()
    cb = COL_HG // W
    spec = lambda k: pl.BlockSpec((ct, W), lambda b, c: (b * nc + c, cb + k))
    return pl.pallas_call(
        functools.partial(_hg_kernel, nchunk=ct // HG_CHUNK),
        grid=(batch, nc),
        in_specs=[spec(0), spec(1), spec(2), spec(3),
                  pl.BlockSpec((1, W), lambda b, c: (0, 0)),
                  pl.BlockSpec((1, HG_DK), lambda b, c: (0, 0)),
                  pl.BlockSpec(sel.shape, lambda b, c: (0, 0)),
                  pl.BlockSpec(msk.shape, lambda b, c: (0, 0, 0))],
        out_specs=pl.BlockSpec((ct, W), lambda b, c: (b * nc + c, 0)),
        out_shape=jax.ShapeDtypeStruct((T, W), BRANCH_DTYPE),
        scratch_shapes=[pltpu.VMEM((HG_HEADS, HG_DK, HG_DK), F32)],
        compiler_params=_cparams(("parallel", "arbitrary")),
    )(proj, proj, proj, proj, lb.reshape(1, W), norm_w.reshape(1, HG_DK), sel, msk)


SSD_PAD = 8


def _ssd_kernel(z_ref, xbc_ref, dt_ref, cw_ref, cb_ref, dtb_ref, a_ref, d_ref, nw_ref, tril3_ref,
                o_ref, pad_ref, st_ref):
    L = SSD_CHUNK
    W = SSD_WIDTH
    GN = 2 * SSD_STATE

    @pl.when(pl.program_id(1) == 0)
    def _():
        pad_ref[0:SSD_PAD, :] = jnp.zeros((SSD_PAD, SSD_CONV_DIM), F32)
        st_ref[...] = jnp.zeros_like(st_ref)

    pad_ref[SSD_PAD:SSD_PAD + L, :] = xbc_ref[...]
    conv = cb_ref[...]
    for k in range(SSD_CONV):
        off = SSD_PAD - (SSD_CONV - 1) + k
        conv = conv + cw_ref[k:k + 1, :] * pad_ref[off:off + L, :]
    pad_ref[0:SSD_PAD, :] = pad_ref[L:L + SSD_PAD, :]
    xa = _silu(conv)
    xs = xa[:, :W]
    bm = xa[:, W:W + GN]
    cm = xa[:, W + GN:W + 2 * GN]
    dtv = dt_ref[...] + dtb_ref[...]
    dt = jnp.maximum(dtv, 0.0) + jnp.log1p(jnp.exp(-jnp.abs(dtv)))
    a = a_ref[...] * dt
    tril3 = tril3_ref[...]
    a_cum = _dot(tril3, _split3(a, 0))
    a_last = a_cum[L - 1:L, :]
    xdt = xs * dt

    lane = lax.broadcasted_iota(jnp.int32, (1, LANES), 1)
    half = [lane < SSD_STATE, lane >= SSD_STATE]
    row = lax.broadcasted_iota(jnp.int32, (L, L), 0)
    col = lax.broadcasted_iota(jnp.int32, (L, L), 1)
    strict = col < row
    causal = col <= row
    bb = bm.astype(BF16)
    cbm = cm.astype(BF16)
    cb_g = [_dot_nt(jnp.where(half[g], cm, 0.0).astype(BF16), bb) for g in range(2)]

    y_parts = []
    for pair in range(SSD_HEADS // 2):
        g = pair // 2
        xp = xdt[:, pair * LANES:(pair + 1) * LANES]
        acc = jnp.zeros((L, LANES), F32)
        for hh in range(2):
            h = 2 * pair + hh
            a_col = a[:, h * SSD_HEAD_DIM:h * SSD_HEAD_DIM + 1]
            diff = _dot(tril3, _split3(jnp.where(strict, a_col, 0.0), 0))
            seg = jnp.exp(jnp.where(causal, diff, NEG_BIG))
            sc = (cb_g[g] * seg).astype(BF16)
            acc = acc + _dot(sc, jnp.where(half[hh], xp, 0.0).astype(BF16))
        y_parts.append(acc)
    y = jnp.concatenate(y_parts, axis=1)

    st = st_ref[...]
    y = y + jnp.exp(a_cum) * _dot(cbm, st.astype(BF16))
    srow = lax.broadcasted_iota(jnp.int32, (GN, W), 0) // SSD_STATE
    scol = lax.broadcasted_iota(jnp.int32, (GN, W), 1) // (W // 2)
    upd = _dot_tn(bb, (jnp.exp(a_last - a_cum) * xdt).astype(BF16))
    st_ref[...] = st * jnp.exp(a_last) + jnp.where(srow == scol, upd, 0.0)

    y = y + d_ref[...] * xs
    y = y * _silu(z_ref[...])
    nw = nw_ref[...]
    outs = []
    for g in range(2):
        yg = y[:, g * (W // 2):(g + 1) * (W // 2)]
        ms = jnp.mean(yg * yg, axis=-1, keepdims=True)
        outs.append(yg * lax.rsqrt(ms + EPS) * nw[:, g * (W // 2):(g + 1) * (W // 2)])
    o_ref[...] = jnp.concatenate(outs, axis=1).astype(o_ref.dtype)


def _ssd(proj, conv_w, conv_b, dt_bias, a_log, d_skip, norm_w, batch, seq):
    T = proj.shape[0]
    L = SSD_CHUNK
    W = SSD_WIDTH
    nc = seq // L
    rep = lambda v: jnp.repeat(v.astype(F32), SSD_HEAD_DIM).reshape(1, W)
    tril = np.tril(np.ones((L, L), np.float32))
    tril3 = jnp.asarray(np.concatenate([tril, tril, tril], 1), dtype=BF16)
    vec = lambda n: pl.BlockSpec((1, n), lambda b, c: (0, 0))
    return pl.pallas_call(
        _ssd_kernel,
        grid=(batch, nc),
        in_specs=[
            pl.BlockSpec((L, W), lambda b, c: (b * nc + c, COL_Z // W)),
            pl.BlockSpec((L, SSD_CONV_DIM), lambda b, c: (b * nc + c, COL_XBC // SSD_CONV_DIM)),
            pl.BlockSpec((L, W), lambda b, c: (b * nc + c, COL_DT // W)),
            pl.BlockSpec((SSD_CONV, SSD_CONV_DIM), lambda b, c: (0, 0)),
            vec(SSD_CONV_DIM), vec(W), vec(W), vec(W), vec(W),
            pl.BlockSpec((L, 3 * L), lambda b, c: (0, 0)),
        ],
        out_specs=pl.BlockSpec((L, W), lambda b, c: (b * nc + c, 0)),
        out_shape=jax.ShapeDtypeStruct((T, W), BRANCH_DTYPE),
        scratch_shapes=[pltpu.VMEM((L + SSD_PAD, SSD_CONV_DIM), F32),
                        pltpu.VMEM((2 * SSD_STATE, W), F32)],
        compiler_params=_cparams(("parallel", "arbitrary")),
    )(proj, proj, proj, conv_w, conv_b.reshape(1, SSD_CONV_DIM), rep(dt_bias),
      rep(-jnp.exp(a_log.astype(F32))), rep(d_skip), norm_w.reshape(1, W), tril3)


def _merge_kernel(ysb_ref, yhg_ref, yssd_ref, g0_ref, g1_ref, g2_ref, x_ref,
                  w0_ref, w1_ref, w2_ref, wo_ref, nw_ref, xo_ref, h_ref):
    m = _sigmoid_pair(g0_ref[...])[0] * _dot(ysb_ref[...], w0_ref[...])
    m = m + _sigmoid_pair(g1_ref[...])[0] * _dot(yhg_ref[...], w1_ref[...])
    m = m + _sigmoid_pair(g2_ref[...])[0] * _dot(yssd_ref[...], w2_ref[...])
    xn = x_ref[...] + _dot(m.astype(BF16), wo_ref[...])
    xo_ref[...] = xn
    ms = jnp.mean(xn * xn, axis=-1, keepdims=True)
    h_ref[...] = (xn * lax.rsqrt(ms + EPS) * nw_ref[...]).astype(BF16)


def _merge(y_sb, y_hg, y_ssd, proj, x, w_sb, w_hg, w_ssd, w_out, norm_w, tm):
    T, D = x.shape
    Wb = y_sb.shape[1]
    yspec = pl.BlockSpec((tm, Wb), lambda i: (i, 0))
    gspec = lambda k: pl.BlockSpec((tm, D), lambda i: (i, COL_GATE // D + k))
    wspec = pl.BlockSpec((Wb, D), lambda i: (0, 0))
    xspec = pl.BlockSpec((tm, D), lambda i: (i, 0))
    return pl.pallas_call(
        _merge_kernel,
        grid=(T // tm,),
        in_specs=[yspec, yspec, yspec, gspec(0), gspec(1), gspec(2), xspec,
                  wspec, wspec, wspec,
                  pl.BlockSpec((D, D), lambda i: (0, 0)),
                  pl.BlockSpec((1, D), lambda i: (0, 0))],
        out_specs=[xspec, xspec],
        out_shape=[jax.ShapeDtypeStruct((T, D), F32), jax.ShapeDtypeStruct((T, D), BF16)],
        compiler_params=_cparams(("parallel",)),
    )(y_sb, y_hg, y_ssd, proj, proj, proj, x, w_sb, w_hg, w_ssd, w_out, norm_w.reshape(1, D))


def _peerq_kernel(h_ref, wq_ref, sk_ref, s1_ref, s2_ref, ht_ref):
    h = h_ref[...]
    q = _dot(h, wq_ref[...]).astype(BF16)
    for hd in range(PEER_HEADS):
        for p, s_ref in enumerate((s1_ref, s2_ref)):
            j = hd * 2 + p
            s_ref[hd] = _dot_nt(sk_ref[j], q[:, j * PEER_NKEYS:(j + 1) * PEER_NKEYS])
    ht_ref[...] = h.astype(F32).T.astype(BF16)


def _peer_query(h2, w_q, sub_keys, tm):
    T, D = h2.shape
    NK = PEER_NKEYS
    sspec = pl.BlockSpec((PEER_HEADS, NK, tm), lambda i: (0, 0, i))
    return pl.pallas_call(
        _peerq_kernel,
        grid=(T // tm,),
        in_specs=[pl.BlockSpec((tm, D), lambda i: (i, 0)),
                  pl.BlockSpec(w_q.shape, lambda i: (0, 0)),
                  pl.BlockSpec(sub_keys.shape, lambda i: (0, 0, 0))],
        out_specs=[sspec, sspec, pl.BlockSpec((None, D, tm), lambda i: (i, 0, 0))],
        out_shape=[jax.ShapeDtypeStruct((PEER_HEADS, NK, T), F32),
                   jax.ShapeDtypeStruct((PEER_HEADS, NK, T), F32),
                   jax.ShapeDtypeStruct((T // tm, D, tm), BF16)],
        compiler_params=_cparams(("parallel",)),
    )(h2, w_q, sub_keys)


def _oddeven_merge(lo, hi, r):
    step = r * 2
    if step < hi - lo:
        yield from _oddeven_merge(lo, hi, step)
        yield from _oddeven_merge(lo + r, hi, step)
        yield from [(i, i + r) for i in range(lo + r, hi - r, step)]
    else:
        yield (lo, lo + r)


def _oddeven_sort(lo, hi):
    if hi - lo >= 1:
        mid = lo + (hi - lo) // 2
        yield from _oddeven_sort(lo, mid)
        yield from _oddeven_sort(mid + 1, hi)
        yield from _oddeven_merge(lo, hi, 1)


SORT16 = tuple(_oddeven_sort(0, PEER_TOPK - 1))


def _cmp_exchange(x, i, j):
    x[i], x[j] = jnp.maximum(x[i], x[j]), jnp.minimum(x[i], x[j])


def _top16(x, n_valid):
    K = PEER_TOPK
    for i, j in SORT16:
        if j < n_valid:
            _cmp_exchange(x, i, j)
    for shift in (4, 2, 1):
        y = [pltpu.roll(v, shift, 0) for v in x]
        x = [jnp.maximum(x[i], y[K - 1 - i]) for i in range(K)]
        d = K // 2
        while d >= 1:
            for i in range(K):
                if i & d == 0:
                    _cmp_exchange(x, i, i + d)
            d //= 2
    return x


def _count_above(v, x):
    assert len(v) == 16
    c8 = v[7] > x
    c4 = jnp.where(c8, v[11], v[3]) > x
    c2 = jnp.where(c8, jnp.where(c4, v[13], v[9]), jnp.where(c4, v[5], v[1])) > x
    hi = jnp.where(c4, jnp.where(c2, v[14], v[12]), jnp.where(c2, v[10], v[8]))
    lo = jnp.where(c4, jnp.where(c2, v[6], v[4]), jnp.where(c2, v[2], v[0]))
    c1 = jnp.where(c8, hi, lo) > x
    count = jnp.where(c8, 8.0, 0.0) + jnp.where(c4, 4.0, 0.0) + jnp.where(c2, 2.0, 0.0)
    return count + jnp.where(c1, 1.0, 0.0) + jnp.where(v[15] > x, 1.0, 0.0)


def _pack_rows(rep, sub):
    out = rep[0]
    for i in range(1, SUBLANES):
        out = jnp.where(sub == i, rep[i], out)
    return out


def _peer_topk_kernel(s1_ref, s2_ref, rank_ref, e2_ref, n_ref, r1_ref):
    for hd in range(s1_ref.shape[0]):
        _peer_topk_head(hd, s1_ref, s2_ref, rank_ref, e2_ref, n_ref, r1_ref)


def _peer_topk_head(hd, s1_ref, s2_ref, rank_ref, e2_ref, n_ref, r1_ref):
    K = PEER_TOPK
    G = PEER_NKEYS // SUBLANES
    tt = s1_ref.shape[2]
    s1 = [s1_ref[hd, g * SUBLANES:(g + 1) * SUBLANES, :] for g in range(G)]
    s2 = [s2_ref[hd, g * SUBLANES:(g + 1) * SUBLANES, :] for g in range(G)]
    v1 = _top16(list(s1), G)
    v2 = _top16(list(s2), G)
    sub = lax.broadcasted_iota(jnp.int32, (SUBLANES, tt), 0)
    v1p = [_pack_rows(v1[:SUBLANES], sub), _pack_rows(v1[SUBLANES:], sub)]
    v2p_hi = _pack_rows(v2[SUBLANES:], sub)
    cand = [v1p[0] + v2[0], v1p[1] + v2[0]] + [v1p[0] + v2[j] for j in range(1, SUBLANES)]
    cand.append(v1[0] + v2p_hi)
    ninf = jnp.full((SUBLANES, tt), -jnp.inf, F32)
    top = _top16(cand + [ninf] * (K - len(cand)), len(cand))
    tau = top[K - 1]
    zsum = jnp.ones_like(tau)
    for r in range(1, K):
        zsum = zsum + jnp.exp(top[r] - top[0])
    cnt = [jnp.zeros((SUBLANES, tt), F32), jnp.zeros((SUBLANES, tt), F32)]
    for j in range(K):
        for k in range(2):
            cnt[k] = cnt[k] + jnp.where(v1p[k] + v2[j] >= tau, 1.0, 0.0)
    n_rep = [jnp.broadcast_to(cnt[i // SUBLANES][i % SUBLANES:i % SUBLANES + 1, :], (SUBLANES, tt))
             for i in range(K)]
    n_rep = [n_rep[0]] + [jnp.where(v1[i] < v1[i - 1], n_rep[i], 0.0) for i in range(1, K)]
    inv_z = 1.0 / zsum
    for g in range(G):
        rows = slice(g * SUBLANES, (g + 1) * SUBLANES)
        n = jnp.zeros((SUBLANES, tt), F32)
        for i in range(K - 1, -1, -1):
            n = jnp.where(s1[g] == v1[i], n_rep[i], n)
        n_ref[hd, rows, :] = n
        r1_ref[hd, rows, :] = jnp.exp(s1[g] - v1[0]) * inv_z
    for g in range(0, G, 2):
        rk, e2 = [], []
        for gg in (g, g + 1):
            rk.append(_count_above(v2, s2[gg]))
            e2.append(jnp.exp(s2[gg] - v2[0]))
        rows = slice(g * SUBLANES, (g + 2) * SUBLANES)
        rank_ref[hd, rows, :] = jnp.concatenate(rk, axis=0).astype(BF16)
        e2_ref[hd, rows, :] = jnp.concatenate(e2, axis=0).astype(BF16)


def _peer_topk(s1t, s2t, tt, hb):
    H, NK, T = s1t.shape
    sspec = pl.BlockSpec((hb, NK, tt), lambda i, h: (h, 0, i))
    return pl.pallas_call(
        _peer_topk_kernel,
        grid=(T // tt, H // hb),
        in_specs=[sspec, sspec],
        out_specs=[sspec, sspec, sspec, sspec],
        out_shape=[jax.ShapeDtypeStruct((H, NK, T), BF16), jax.ShapeDtypeStruct((H, NK, T), BF16),
                   jax.ShapeDtypeStruct((H, NK, T), F32), jax.ShapeDtypeStruct((H, NK, T), F32)],
        compiler_params=_cparams(("parallel", "parallel")),
    )(s1t, s2t)


def _peer_kernel(x_ref, ht_ref, htn_ref, rank_ref, e2_ref, n_ref, r1_ref, u0_ref, u1_ref, un_ref, v_ref,
                 fw_ref, o_ref, act2, p2, acc_ref, *, eb, tt, final):
    blk = pl.program_id(1)
    NK = PEER_NKEYS
    assert eb // NK == SUBLANES
    zero = jnp.zeros((), BF16)
    act_ref = act2.at[blk % 2]
    p_ref = p2.at[blk % 2]

    def build_p(tc):
        a0 = pl.multiple_of(blk * SUBLANES, SUBLANES)
        ts = slice(tc * LANES, (tc + 1) * LANES)
        n8 = [n_ref[h, pl.ds(a0, SUBLANES), ts] for h in range(PEER_HEADS)]
        r8 = [r1_ref[h, pl.ds(a0, SUBLANES), ts] for h in range(PEER_HEADS)]
        for ai in range(SUBLANES):
            w = None
            for h in range(PEER_HEADS):
                n_row = n8[h][ai:ai + 1, :].astype(BF16)
                r_row = r8[h][ai:ai + 1, :].astype(BF16)
                term = jnp.where(rank_ref[h, :, ts] < n_row, e2_ref[h, :, ts] * r_row, zero)
                w = term if w is None else w + term
            act = act_ref[ai * NK:(ai + 1) * NK, ts]
            gelu = 0.5 * act * (1.0 + lax.erf(act * (2.0 ** -0.5)))
            p_ref[ai * NK:(ai + 1) * NK, ts] = w * gelu.astype(BF16)

    @pl.when(blk == 0)
    def _():
        acc_ref[...] = jnp.zeros_like(acc_ref)

    @pl.when((blk == 0) & (pl.program_id(0) == 0))
    def _():
        act2[0] = _dot(u0_ref[...], ht_ref[...])
        act2[1] = _dot(u1_ref[...], ht_ref[...])

    half = tt // 2
    per_half = half // LANES
    for c in range(2):
        cs = slice(c * half, (c + 1) * half)
        for tc in range(c * per_half, (c + 1) * per_half):
            build_p(tc)
        acc_ref[:, cs] += _dot(v_ref[...], p_ref[:, cs])
        act_ref[:, cs] = _dot(un_ref[...], htn_ref[:, cs])

    @pl.when(blk == pl.num_programs(1) - 1)
    def _():
        y = x_ref[...] + acc_ref[...].T
        if final:
            ms = jnp.mean(y * y, axis=-1, keepdims=True)
            y = y * lax.rsqrt(ms + EPS) * fw_ref[...]
        o_ref[...] = y


def _peer(x, h2t, rank2, e2, n1, r1, u, vt, final_w, tt, eb, final):
    T, D = x.shape
    H, NK = PEER_HEADS, PEER_NKEYS
    ne = vt.shape[0]
    assert h2t.shape == (T // tt, D, tt) and vt.shape == (ne, D, eb) and u.shape == (ne * eb, D)
    sspec = pl.BlockSpec((H, NK, tt), lambda i, e: (0, 0, i))
    once = pl.Buffered(1)
    nt = T // tt
    return pl.pallas_call(
        functools.partial(_peer_kernel, eb=eb, tt=tt, final=final),
        grid=(nt, ne),
        in_specs=[pl.BlockSpec((tt, D), lambda i, e: (i, 0)),
                  pl.BlockSpec((None, D, tt), lambda i, e: (i, 0, 0)),
                  pl.BlockSpec((None, D, tt), lambda i, e: (jnp.minimum(i + (e + 2) // ne, nt - 1), 0, 0)),
                  sspec, sspec, sspec, sspec,
                  pl.BlockSpec((eb, D), lambda i, e: (0, 0), pipeline_mode=once),
                  pl.BlockSpec((eb, D), lambda i, e: (1, 0), pipeline_mode=once),
                  pl.BlockSpec((eb, D), lambda i, e: ((e + 2) % ne, 0)),
                  pl.BlockSpec((None, D, eb), lambda i, e: (e, 0, 0)),
                  pl.BlockSpec((1, D), lambda i, e: (0, 0))],
        out_specs=pl.BlockSpec((tt, D), lambda i, e: (i, 0)),
        out_shape=jax.ShapeDtypeStruct((T, D), F32),
        scratch_shapes=[pltpu.VMEM((2, eb, tt), F32), pltpu.VMEM((2, eb, tt), BF16),
                        pltpu.VMEM((D, tt), F32)],
        compiler_params=_cparams(("arbitrary", "arbitrary")),
    )(x, h2t, h2t, rank2, e2, n1, r1, u, u, u, vt, final_w.reshape(1, D))


def _layer(x, batch, seq, lb, p, final_w, final, tiles):
    proj = _inproj(x, p["norm1_w"], p["w_in"], tiles["in_tm"], tiles["in_tn"])
    y_sb = _sb_attention(proj, batch, seq, tiles["sb_blk"], tiles["sb_nsub"])
    y_hg = _hgrn2(proj, lb, p["hg_norm_w"], batch, seq, tiles["hg_ct"])
    y_ssd = _ssd(proj, p["ssd_conv_w"], p["ssd_conv_b"], p["ssd_dt_bias"], p["ssd_a_log"],
                 p["ssd_d"], p["ssd_norm_w"], batch, seq)
    x, h2 = _merge(y_sb, y_hg, y_ssd, proj, x, p["w_branch_sb"], p["w_branch_hg"],
                   p["w_branch_ssd"], p["w_out"], p["norm2_w"], tiles["mg_tm"])
    s1t, s2t, h2t = _peer_query(h2, p["peer_w_q"], p["peer_sub_keys"], tiles["pq_tm"])
    rank2, e2, n1, r1 = _peer_topk(s1t, s2t, tiles["tk_tt"], tiles["tk_hb"])
    return _peer(x, h2t, rank2, e2, n1, r1, p["peer_u"], p["peer_vt"], final_w,
                 tiles["pe_tt"], tiles["pe_eb"], final)


TILES = dict(in_tm=2048, in_tn=768, sb_blk=128, sb_nsub=2, hg_ct=256, mg_tm=256, pq_tm=512,
             tk_tt=256, tk_hb=2, pe_tt=512, pe_eb=1024)


def _prep_w_in(w):
    c = np.cumsum([0, 512, 512, 512, 512, 512, 512, 512, 512, 768, 8, 1024, 1024, 1024])
    dt_rep = jnp.repeat(w[:, c[9]:c[10]], SSD_HEAD_DIM, axis=1)
    return jnp.concatenate([w[:, c[3]:c[7]], w[:, c[7]:c[8]], dt_rep, w[:, c[10]:c[13]],
                            w[:, c[8]:c[9]], w[:, c[0]:c[3]]], axis=1).astype(BF16)


def _trunk(x, norm1_w, w_in, hg_lb_logits, hg_norm_w, ssd_conv_w, ssd_conv_b, ssd_dt_bias,
           ssd_a_log, ssd_d, ssd_norm_w, w_branch_sb, w_branch_hg, w_branch_ssd, w_out,
           norm2_w, peer_w_q, peer_sub_keys, peer_u, peer_v, final_norm_w, tiles):
    batch, seq, d = x.shape
    depth = w_in.shape[0]
    gamma = jax.nn.softmax(hg_lb_logits.astype(F32), axis=0)
    lower_bounds = jnp.cumsum(gamma, axis=0) - gamma[0]
    xt = x.reshape(batch * seq, d)
    for l in range(depth):
        p = dict(
            norm1_w=norm1_w[l], w_in=_prep_w_in(w_in[l]), hg_norm_w=hg_norm_w[l],
            ssd_conv_w=ssd_conv_w[l], ssd_conv_b=ssd_conv_b[l], ssd_dt_bias=ssd_dt_bias[l],
            ssd_a_log=ssd_a_log[l], ssd_d=ssd_d[l], ssd_norm_w=ssd_norm_w[l],
            w_branch_sb=w_branch_sb[l].astype(BF16), w_branch_hg=w_branch_hg[l].astype(BF16),
            w_branch_ssd=w_branch_ssd[l].astype(BF16), w_out=w_out[l].astype(BF16),
            norm2_w=norm2_w[l], peer_w_q=peer_w_q[l].astype(BF16),
            peer_sub_keys=peer_sub_keys[l].reshape(2 * PEER_HEADS, PEER_NKEYS, -1).astype(BF16),
            peer_u=peer_u[l].astype(BF16),
            peer_vt=peer_v[l].reshape(-1, tiles["pe_eb"], d).transpose(0, 2, 1).astype(BF16))
        xt = _layer(xt, batch, seq, lower_bounds[l], p, final_norm_w, l == depth - 1, tiles)
    return xt.reshape(batch, seq, d)


def kernel(x, norm1_w, w_in, hg_lb_logits, hg_norm_w, ssd_conv_w, ssd_conv_b, ssd_dt_bias, ssd_a_log, ssd_d, ssd_norm_w, w_branch_sb, w_branch_hg, w_branch_ssd, w_out, norm2_w, peer_w_q, peer_sub_keys, peer_u, peer_v, final_norm_w):
    return _trunk(x, norm1_w, w_in, hg_lb_logits, hg_norm_w, ssd_conv_w, ssd_conv_b, ssd_dt_bias,
                  ssd_a_log, ssd_d, ssd_norm_w, w_branch_sb, w_branch_hg, w_branch_ssd, w_out,
                  norm2_w, peer_w_q, peer_sub_keys, peer_u, peer_v, final_norm_w, TILES)
```

```python
import functools

import numpy as np
import jax
import jax.numpy as jnp
from jax import lax
from jax.experimental import pallas as pl
from jax.experimental.pallas import tpu as pltpu

F32 = jnp.float32
BF16 = jnp.bfloat16
BRANCH_DTYPE = BF16

D_MODEL = 1024
SB_HEAD_DIM = 64
SB_WIDTH = 512
HG_HEADS = 4
HG_DK = 128
HG_CHUNK = 64
HG_F_FLOOR = 1e-30
SSD_HEADS = 8
SSD_HEAD_DIM = 64
SSD_WIDTH = 512
SSD_STATE = 64
SSD_CONV = 4
SSD_CHUNK = 128
SSD_CONV_DIM = 768
PEER_HEADS = 8
PEER_NKEYS = 128
PEER_TOPK = 16
EPS = 1e-6
NEG_BIG = -1e30

LANES = 128
SUBLANES = 8
VMEM_LIMIT = 56 * 1024 * 1024

COL_HG = 0
COL_Z = 2048
COL_DT = 2560
COL_GATE = 3072
COL_XBC = 6144
COL_SB = 6912
IN_COLS_PADDED = 8448


def _cparams(sem):
    return pltpu.CompilerParams(dimension_semantics=sem, vmem_limit_bytes=VMEM_LIMIT)


def _dot(a, b):
    return jnp.dot(a, b, preferred_element_type=F32)


def _dot_nt(a, b):
    return lax.dot_general(a, b, (((1,), (1,)), ((), ())), preferred_element_type=F32)


def _dot_tn(a, b):
    return lax.dot_general(a, b, (((0,), (0,)), ((), ())), preferred_element_type=F32)


def _sigmoid_pair(x):
    e = jnp.exp(-jnp.abs(x))
    inv = 1.0 / (1.0 + e)
    small = e * inv
    pos = x >= 0
    return jnp.where(pos, inv, small), jnp.where(pos, small, inv)


def _silu(x):
    return x * _sigmoid_pair(x)[0]


def _split3(x, axis):
    hi = x.astype(BF16)
    r1 = x - hi.astype(F32)
    mid = r1.astype(BF16)
    lo = (r1 - mid.astype(F32)).astype(BF16)
    return jnp.concatenate([hi, mid, lo], axis=axis)


def _inproj_kernel(x_ref, nw_ref, w_ref, o_ref, h_ref):
    @pl.when(pl.program_id(1) == 0)
    def _():
        x = x_ref[...]
        ms = jnp.mean(x * x, axis=-1, keepdims=True)
        h_ref[...] = (x * lax.rsqrt(ms + EPS) * nw_ref[...]).astype(BF16)

    o_ref[...] = _dot(h_ref[...], w_ref[...])


def _inproj(x, norm_w, w, tm, tn):
    T, D = x.shape
    N = w.shape[1]
    return pl.pallas_call(
        _inproj_kernel,
        grid=(T // tm, N // tn),
        in_specs=[
            pl.BlockSpec((tm, D), lambda i, j: (i, 0)),
            pl.BlockSpec((1, D), lambda i, j: (0, 0)),
            pl.BlockSpec((D, tn), lambda i, j: (0, j)),
        ],
        out_specs=pl.BlockSpec((tm, tn), lambda i, j: (i, j)),
        out_shape=jax.ShapeDtypeStruct((T, N), F32),
        scratch_shapes=[pltpu.VMEM((tm, D), BF16)],
        compiler_params=_cparams(("parallel", "arbitrary")),
    )(x, norm_w.reshape(1, D), w)


SB_EXIT = -110.0


def _sb_kernel(q_ref, k_ref, v_ref, mcat_ref, o_ref, c_ref, acc_ref, *, blk, nsub):
    nseq = q_ref.shape[0]
    qi = pl.program_id(1)
    lane = lax.broadcasted_iota(jnp.int32, (1, LANES), 1)
    first_head = lane < SB_HEAD_DIM
    mcat = mcat_ref[...]
    qs = []
    for g in range(nseq):
        qparts = []
        for s in range(nsub):
            q = q_ref[g, s * blk:(s + 1) * blk, :] * (SB_HEAD_DIM ** -0.5)
            qparts += [jnp.where(first_head, q, 0.0), jnp.where(first_head, 0.0, q)]
        qs.append(jnp.concatenate(qparts, axis=0).astype(BF16))
    c_ref[...] = jnp.zeros_like(c_ref)
    acc_ref[...] = jnp.zeros_like(acc_ref)

    def visit(kb, s0, diag):
        r0 = s0 * 2 * blk
        n = (nsub - s0) * 2 * blk
        ks = pl.multiple_of(kb * blk, blk)
        z = jnp.concatenate([_dot_nt(qs[g][r0:], k_ref[g, pl.ds(ks, blk), :].astype(BF16))
                             for g in range(nseq)], axis=0)
        t = jnp.log1p(jnp.exp(-jnp.abs(z)))
        log_beta = jnp.minimum(z, 0.0) - t
        log_rest = jnp.minimum(-z, 0.0) - t
        if diag:
            row = lax.rem(lax.broadcasted_iota(jnp.int32, z.shape, 0), n)
            col = lax.broadcasted_iota(jnp.int32, z.shape, 1)
            before = (row >= 2 * blk) | (col < (row & (blk - 1)))
            log_rest = jnp.where(before, log_rest, 0.0)
        hi = log_rest.astype(BF16)
        lo = (log_rest - hi.astype(F32)).astype(BF16)
        cs = _dot(jnp.concatenate([hi, lo], axis=1), mcat)
        carry = jnp.concatenate([c_ref[g, r0:, :] for g in range(nseq)], axis=0)
        a = jnp.exp(log_beta + cs[:, :blk] + carry)
        if diag:
            a = jnp.where(before, a, 0.0)
        carry = carry + cs[:, blk:]
        a = a.astype(BF16)
        for g in range(nseq):
            c_ref[g, r0:, :] = carry[g * n:(g + 1) * n]
            av = _dot(a[g * n:(g + 1) * n], v_ref[g, pl.ds(ks, blk), :].astype(BF16))
            for s in range(s0, nsub):
                i = (s - s0) * 2 * blk
                acc_ref[g, s * blk:(s + 1) * blk, :] += jnp.where(first_head, av[i:i + blk], av[i + blk:i + 2 * blk])
        return carry

    for j in range(nsub - 1, -1, -1):
        carry = visit(qi * nsub + j, j, True)

    def body(state):
        kb, _ = state
        return kb - 1, jnp.max(visit(kb, 0, False))

    lax.while_loop(lambda st: (st[0] >= 0) & (st[1] > SB_EXIT), body,
                   (qi * nsub - 1, jnp.max(carry)))
    o_ref[...] = acc_ref[...].astype(o_ref.dtype)


def _sb_consts(blk):
    r = np.arange(2 * blk)[:, None] % blk
    c = np.arange(2 * blk)[None, :]
    return jnp.asarray(((c >= blk) | (r > c)).astype(np.float32), dtype=BF16)


def _sb_attention(proj, batch, seq, blk, nsub):
    T, ncol = proj.shape
    tq = blk * nsub
    cb = COL_SB // LANES
    npair = SB_WIDTH // LANES
    proj3 = proj.reshape(batch, seq, ncol)
    out = pl.pallas_call(
        functools.partial(_sb_kernel, blk=blk, nsub=nsub),
        grid=(npair, seq // tq),
        in_specs=[
            pl.BlockSpec((batch, tq, LANES), lambda p, i: (0, i, cb + p)),
            pl.BlockSpec((batch, seq, LANES), lambda p, i: (0, 0, cb + npair + p)),
            pl.BlockSpec((batch, seq, LANES), lambda p, i: (0, 0, cb + 2 * npair + p)),
            pl.BlockSpec((2 * blk, 2 * blk), lambda p, i: (0, 0)),
        ],
        out_specs=pl.BlockSpec((batch, tq, LANES), lambda p, i: (0, i, p)),
        out_shape=jax.ShapeDtypeStruct((batch, seq, SB_WIDTH), BRANCH_DTYPE),
        scratch_shapes=[pltpu.VMEM((batch, nsub * 2 * blk, blk), F32), pltpu.VMEM((batch, tq, LANES), F32)],
        compiler_params=_cparams(("parallel", "arbitrary")),
    )(proj3, proj3, proj3, _sb_consts(blk))
    return out.reshape(T, SB_WIDTH)


HG_LEVELS = (32, 16, 8, 4, 2, 1)


def _hg_consts():
    C = HG_CHUNK
    i = np.arange(C)
    sels = [i[None, :] <= i[:, None]]
    masks = []
    for m in HG_LEVELS:
        g = (i // (2 * m)) * (2 * m) + m - 1
        sels.append(i[None, :] <= g[:, None])
        same = (i[:, None] // (2 * m)) == (i[None, :] // (2 * m))
        upper_t = (i[:, None] % (2 * m)) >= m
        lower_s = (i[None, :] % (2 * m)) < m
        masks.append(same & upper_t & lower_s)
    masks.append(np.eye(C, dtype=bool))
    big = np.concatenate(sels, 0).astype(np.float32)
    big3 = np.concatenate([big, big, big], 1)
    return jnp.asarray(big3, dtype=BF16), jnp.asarray(np.stack(masks).astype(np.float32))


def _hg_kernel(f_ref, q_ref, i_ref, g_ref, lb_ref, nw_ref, sel_ref, msk_ref, o_ref, st_ref, *, nchunk):
    C = HG_CHUNK

    @pl.when(pl.program_id(1) == 0)
    def _():
        st_ref[...] = jnp.zeros_like(st_ref)

    sel = sel_ref[...]
    nw = nw_ref[...]
    lb = lb_ref[...]
    nlvl = len(HG_LEVELS)
    for c in range(nchunk):
        rows = slice(c * C, (c + 1) * C)
        sp, sn = _sigmoid_pair(f_ref[rows, :])
        f = lb + (1.0 - lb) * sp
        log_f = jnp.log(jnp.maximum(f, HG_F_FLOOR))
        key = (1.0 - lb) * sn
        q = _silu(q_ref[rows, :]) * (HG_DK ** -0.5)
        val = i_ref[rows, :].astype(BF16)
        br = _dot(sel, _split3(log_f, 0))
        b = br[:C]
        qk = []
        for lvl in range(nlvl):
            r = br[(lvl + 1) * C:(lvl + 2) * C]
            qk.append(((q * jnp.exp(jnp.minimum(b - r, 0.0))).astype(BF16),
                       (key * jnp.exp(jnp.minimum(r - b, 0.0))).astype(BF16)))
        qk.append((q.astype(BF16), key.astype(BF16)))
        q_in = (q * jnp.exp(b)).astype(BF16)
        b_last = b[C - 1:C, :]
        kt = (key * jnp.exp(b_last - b)).astype(BF16)
        decay = jnp.exp(b_last)
        outs = []
        for h in range(HG_HEADS):
            cols = slice(h * HG_DK, (h + 1) * HG_DK)
            s = msk_ref[0] * _dot_nt(qk[0][0][:, cols], qk[0][1][:, cols])
            for lvl in range(1, nlvl + 1):
                s = s + msk_ref[lvl] * _dot_nt(qk[lvl][0][:, cols], qk[lvl][1][:, cols])
            st = st_ref[h]
            o = _dot(s.astype(BF16), val[:, cols]) + _dot_nt(q_in[:, cols], st.astype(BF16))
            st_ref[h] = st * decay[:, cols] + _dot_tn(val[:, cols], kt[:, cols])
            ms = jnp.mean(o * o, axis=-1, keepdims=True)
            outs.append(o * lax.rsqrt(ms + EPS) * nw)
        o_ref[rows, :] = (jnp.concatenate(outs, axis=1) * _silu(g_ref[rows, :])).astype(o_ref.dtype)


def _hgrn2(proj, lb, norm_w, batch, seq, ct):
    T = proj.shape[0]
    W = HG_HEADS * HG_DK
    nc = seq // ct
    sel, msk = _hg_consts()
    cb = COL_HG // W
    spec = lambda k: pl.BlockSpec((ct, W), lambda b, c: (b * nc + c, cb + k))
    return pl.pallas_call(
        functools.partial(_hg_kernel, nchunk=ct // HG_CHUNK),
        grid=(batch, nc),
        in_specs=[spec(0), spec(1), spec(2), spec(3),
                  pl.BlockSpec((1, W), lambda b, c: (0, 0)),
                  pl.BlockSpec((1, HG_DK), lambda b, c: (0, 0)),
                  pl.BlockSpec(sel.shape, lambda b, c: (0, 0)),
                  pl.BlockSpec(msk.shape, lambda b, c: (0, 0, 0))],
        out_specs=pl.BlockSpec((ct, W), lambda b, c: (b * nc + c, 0)),
        out_shape=jax.ShapeDtypeStruct((T, W), BRANCH_DTYPE),
        scratch_shapes=[pltpu.VMEM((HG_HEADS, HG_DK, HG_DK), F32)],
        compiler_params=_cparams(("parallel", "arbitrary")),
    )(proj, proj, proj, proj, lb.reshape(1, W), norm_w.reshape(1, HG_DK), sel, msk)


SSD_PAD = 8


def _ssd_kernel(z_ref, xbc_ref, dt_ref, cw_ref, cb_ref, dtb_ref, a_ref, d_ref, nw_ref, tril3_ref,
                o_ref, pad_ref, st_ref):
    L = SSD_CHUNK
    W = SSD_WIDTH
    GN = 2 * SSD_STATE

    @pl.when(pl.program_id(1) == 0)
    def _():
        pad_ref[0:SSD_PAD, :] = jnp.zeros((SSD_PAD, SSD_CONV_DIM), F32)
        st_ref[...] = jnp.zeros_like(st_ref)

    pad_ref[SSD_PAD:SSD_PAD + L, :] = xbc_ref[...]
    conv = cb_ref[...]
    for k in range(SSD_CONV):
        off = SSD_PAD - (SSD_CONV - 1) + k
        conv = conv + cw_ref[k:k + 1, :] * pad_ref[off:off + L, :]
    pad_ref[0:SSD_PAD, :] = pad_ref[L:L + SSD_PAD, :]
    xa = _silu(conv)
    xs = xa[:, :W]
    bm = xa[:, W:W + GN]
    cm = xa[:, W + GN:W + 2 * GN]
    dtv = dt_ref[...] + dtb_ref[...]
    dt = jnp.maximum(dtv, 0.0) + jnp.log1p(jnp.exp(-jnp.abs(dtv)))
    a = a_ref[...] * dt
    tril3 = tril3_ref[...]
    a_cum = _dot(tril3, _split3(a, 0))
    a_last = a_cum[L - 1:L, :]
    xdt = xs * dt

    lane = lax.broadcasted_iota(jnp.int32, (1, LANES), 1)
    half = [lane < SSD_STATE, lane >= SSD_STATE]
    row = lax.broadcasted_iota(jnp.int32, (L, L), 0)
    col = lax.broadcasted_iota(jnp.int32, (L, L), 1)
    strict = col < row
    causal = col <= row
    bb = bm.astype(BF16)
    cbm = cm.astype(BF16)
    cb_g = [_dot_nt(jnp.where(half[g], cm, 0.0).astype(BF16), bb) for g in range(2)]

    y_parts = []
    for pair in range(SSD_HEADS // 2):
        g = pair // 2
        xp = xdt[:, pair * LANES:(pair + 1) * LANES]
        acc = jnp.zeros((L, LANES), F32)
        for hh in range(2):
            h = 2 * pair + hh
            a_col = a[:, h * SSD_HEAD_DIM:h * SSD_HEAD_DIM + 1]
            diff = _dot(tril3, _split3(jnp.where(strict, a_col, 0.0), 0))
            seg = jnp.exp(jnp.where(causal, diff, NEG_BIG))
            sc = (cb_g[g] * seg).astype(BF16)
            acc = acc + _dot(sc, jnp.where(half[hh], xp, 0.0).astype(BF16))
        y_parts.append(acc)
    y = jnp.concatenate(y_parts, axis=1)

    st = st_ref[...]
    y = y + jnp.exp(a_cum) * _dot(cbm, st.astype(BF16))
    srow = lax.broadcasted_iota(jnp.int32, (GN, W), 0) // SSD_STATE
    scol = lax.broadcasted_iota(jnp.int32, (GN, W), 1) // (W // 2)
    upd = _dot_tn(bb, (jnp.exp(a_last - a_cum) * xdt).astype(BF16))
    st_ref[...] = st * jnp.exp(a_last) + jnp.where(srow == scol, upd, 0.0)

    y = y + d_ref[...] * xs
    y = y * _silu(z_ref[...])
    nw = nw_ref[...]
    outs = []
    for g in range(2):
        yg = y[:, g * (W // 2):(g + 1) * (W // 2)]
        ms = jnp.mean(yg * yg, axis=-1, keepdims=True)
        outs.append(yg * lax.rsqrt(ms + EPS) * nw[:, g * (W // 2):(g + 1) * (W // 2)])
    o_ref[...] = jnp.concatenate(outs, axis=1).astype(o_ref.dtype)


def _ssd(proj, conv_w, conv_b, dt_bias, a_log, d_skip, norm_w, batch, seq):
    T = proj.shape[0]
    L = SSD_CHUNK
    W = SSD_WIDTH
    nc = seq // L
    rep = lambda v: jnp.repeat(v.astype(F32), SSD_HEAD_DIM).reshape(1, W)
    tril = np.tril(np.ones((L, L), np.float32))
    tril3 = jnp.asarray(np.concatenate([tril, tril, tril], 1), dtype=BF16)
    vec = lambda n: pl.BlockSpec((1, n), lambda b, c: (0, 0))
    return pl.pallas_call(
        _ssd_kernel,
        grid=(batch, nc),
        in_specs=[
            pl.BlockSpec((L, W), lambda b, c: (b * nc + c, COL_Z // W)),
            pl.BlockSpec((L, SSD_CONV_DIM), lambda b, c: (b * nc + c, COL_XBC // SSD_CONV_DIM)),
            pl.BlockSpec((L, W), lambda b, c: (b * nc + c, COL_DT // W)),
            pl.BlockSpec((SSD_CONV, SSD_CONV_DIM), lambda b, c: (0, 0)),
            vec(SSD_CONV_DIM), vec(W), vec(W), vec(W), vec(W),
            pl.BlockSpec((L, 3 * L), lambda b, c: (0, 0)),
        ],
        out_specs=pl.BlockSpec((L, W), lambda b, c: (b * nc + c, 0)),
        out_shape=jax.ShapeDtypeStruct((T, W), BRANCH_DTYPE),
        scratch_shapes=[pltpu.VMEM((L + SSD_PAD, SSD_CONV_DIM), F32),
                        pltpu.VMEM((2 * SSD_STATE, W), F32)],
        compiler_params=_cparams(("parallel", "arbitrary")),
    )(proj, proj, proj, conv_w, conv_b.reshape(1, SSD_CONV_DIM), rep(dt_bias),
      rep(-jnp.exp(a_log.astype(F32))), rep(d_skip), norm_w.reshape(1, W), tril3)


def _merge_kernel(ysb_ref, yhg_ref, yssd_ref, g0_ref, g1_ref, g2_ref, x_ref,
                  w0_ref, w1_ref, w2_ref, wo_ref, nw_ref, xo_ref, h_ref):
    m = _sigmoid_pair(g0_ref[...])[0] * _dot(ysb_ref[...], w0_ref[...])
    m = m + _sigmoid_pair(g1_ref[...])[0] * _dot(yhg_ref[...], w1_ref[...])
    m = m + _sigmoid_pair(g2_ref[...])[0] * _dot(yssd_ref[...], w2_ref[...])
    xn = x_ref[...] + _dot(m.astype(BF16), wo_ref[...])
    xo_ref[...] = xn
    ms = jnp.mean(xn * xn, axis=-1, keepdims=True)
    h_ref[...] = (xn * lax.rsqrt(ms + EPS) * nw_ref[...]).astype(BF16)


def _merge(y_sb, y_hg, y_ssd, proj, x, w_sb, w_hg, w_ssd, w_out, norm_w, tm):
    T, D = x.shape
    Wb = y_sb.shape[1]
    yspec = pl.BlockSpec((tm, Wb), lambda i: (i, 0))
    gspec = lambda k: pl.BlockSpec((tm, D), lambda i: (i, COL_GATE // D + k))
    wspec = pl.BlockSpec((Wb, D), lambda i: (0, 0))
    xspec = pl.BlockSpec((tm, D), lambda i: (i, 0))
    return pl.pallas_call(
        _merge_kernel,
        grid=(T // tm,),
        in_specs=[yspec, yspec, yspec, gspec(0), gspec(1), gspec(2), xspec,
                  wspec, wspec, wspec,
                  pl.BlockSpec((D, D), lambda i: (0, 0)),
                  pl.BlockSpec((1, D), lambda i: (0, 0))],
        out_specs=[xspec, xspec],
        out_shape=[jax.ShapeDtypeStruct((T, D), F32), jax.ShapeDtypeStruct((T, D), BF16)],
        compiler_params=_cparams(("parallel",)),
    )(y_sb, y_hg, y_ssd, proj, proj, proj, x, w_sb, w_hg, w_ssd, w_out, norm_w.reshape(1, D))


def _peerq_kernel(h_ref, wq_ref, sk_ref, s1_ref, s2_ref, ht_ref):
    h = h_ref[...]
    q = _dot(h, wq_ref[...]).astype(BF16)
    for hd in range(PEER_HEADS):
        for p, s_ref in enumerate((s1_ref, s2_ref)):
            j = hd * 2 + p
            s_ref[hd] = _dot_nt(sk_ref[j], q[:, j * PEER_NKEYS:(j + 1) * PEER_NKEYS])
    ht_ref[...] = h.astype(F32).T.astype(BF16)


def _peer_query(h2, w_q, sub_keys, tm):
    T, D = h2.shape
    NK = PEER_NKEYS
    sspec = pl.BlockSpec((PEER_HEADS, NK, tm), lambda i: (0, 0, i))
    return pl.pallas_call(
        _peerq_kernel,
        grid=(T // tm,),
        in_specs=[pl.BlockSpec((tm, D), lambda i: (i, 0)),
                  pl.BlockSpec(w_q.shape, lambda i: (0, 0)),
                  pl.BlockSpec(sub_keys.shape, lambda i: (0, 0, 0))],
        out_specs=[sspec, sspec, pl.BlockSpec((None, D, tm), lambda i: (i, 0, 0))],
        out_shape=[jax.ShapeDtypeStruct((PEER_HEADS, NK, T), F32),
                   jax.ShapeDtypeStruct((PEER_HEADS, NK, T), F32),
                   jax.ShapeDtypeStruct((T // tm, D, tm), BF16)],
        compiler_params=_cparams(("parallel",)),
    )(h2, w_q, sub_keys)


def _oddeven_merge(lo, hi, r):
    step = r * 2
    if step < hi - lo:
        yield from _oddeven_merge(lo, hi, step)
        yield from _oddeven_merge(lo + r, hi, step)
        yield from [(i, i + r) for i in range(lo + r, hi - r, step)]
    else:
        yield (lo, lo + r)


def _oddeven_sort(lo, hi):
    if hi - lo >= 1:
        mid = lo + (hi - lo) // 2
        yield from _oddeven_sort(lo, mid)
        yield from _oddeven_sort(mid + 1, hi)
        yield from _oddeven_merge(lo, hi, 1)


SORT16 = tuple(_oddeven_sort(0, PEER_TOPK - 1))


def _cmp_exchange(x, i, j):
    x[i], x[j] = jnp.maximum(x[i], x[j]), jnp.minimum(x[i], x[j])


def _top16(x, n_valid):
    K = PEER_TOPK
    for i, j in SORT16:
        if j < n_valid:
            _cmp_exchange(x, i, j)
    for shift in (4, 2, 1):
        y = [pltpu.roll(v, shift, 0) for v in x]
        x = [jnp.maximum(x[i], y[K - 1 - i]) for i in range(K)]
        d = K // 2
        while d >= 1:
            for i in range(K):
                if i & d == 0:
                    _cmp_exchange(x, i, i + d)
            d //= 2
    return x


def _count_above(v, x):
    assert len(v) == 16
    c8 = v[7] > x
    c4 = jnp.where(c8, v[11], v[3]) > x
    c2 = jnp.where(c8, jnp.where(c4, v[13], v[9]), jnp.where(c4, v[5], v[1])) > x
    hi = jnp.where(c4, jnp.where(c2, v[14], v[12]), jnp.where(c2, v[10], v[8]))
    lo = jnp.where(c4, jnp.where(c2, v[6], v[4]), jnp.where(c2, v[2], v[0]))
    c1 = jnp.where(c8, hi, lo) > x
    count = jnp.where(c8, 8.0, 0.0) + jnp.where(c4, 4.0, 0.0) + jnp.where(c2, 2.0, 0.0)
    return count + jnp.where(c1, 1.0, 0.0) + jnp.where(v[15] > x, 1.0, 0.0)


def _pack_rows(rep, sub):
    out = rep[0]
    for i in range(1, SUBLANES):
        out = jnp.where(sub == i, rep[i], out)
    return out


def _peer_topk_kernel(s1_ref, s2_ref, rank_ref, e2_ref, n_ref, r1_ref):
    for hd in range(s1_ref.shape[0]):
        _peer_topk_head(hd, s1_ref, s2_ref, rank_ref, e2_ref, n_ref, r1_ref)


def _peer_topk_head(hd, s1_ref, s2_ref, rank_ref, e2_ref, n_ref, r1_ref):
    K = PEER_TOPK
    G = PEER_NKEYS // SUBLANES
    tt = s1_ref.shape[2]
    s1 = [s1_ref[hd, g * SUBLANES:(g + 1) * SUBLANES, :] for g in range(G)]
    s2 = [s2_ref[hd, g * SUBLANES:(g + 1) * SUBLANES, :] for g in range(G)]
    v1 = _top16(list(s1), G)
    v2 = _top16(list(s2), G)
    sub = lax.broadcasted_iota(jnp.int32, (SUBLANES, tt), 0)
    v1p = [_pack_rows(v1[:SUBLANES], sub), _pack_rows(v1[SUBLANES:], sub)]
    v2p_hi = _pack_rows(v2[SUBLANES:], sub)
    cand = [v1p[0] + v2[0], v1p[1] + v2[0]] + [v1p[0] + v2[j] for j in range(1, SUBLANES)]
    cand.append(v1[0] + v2p_hi)
    ninf = jnp.full((SUBLANES, tt), -jnp.inf, F32)
    top = _top16(cand + [ninf] * (K - len(cand)), len(cand))
    tau = top[K - 1]
    zsum = jnp.ones_like(tau)
    for r in range(1, K):
        zsum = zsum + jnp.exp(top[r] - top[0])
    cnt = [jnp.zeros((SUBLANES, tt), F32), jnp.zeros((SUBLANES, tt), F32)]
    for j in range(K):
        for k in range(2):
            cnt[k] = cnt[k] + jnp.where(v1p[k] + v2[j] >= tau, 1.0, 0.0)
    n_rep = [jnp.broadcast_to(cnt[i // SUBLANES][i % SUBLANES:i % SUBLANES + 1, :], (SUBLANES, tt))
             for i in range(K)]
    n_rep = [n_rep[0]] + [jnp.where(v1[i] < v1[i - 1], n_rep[i], 0.0) for i in range(1, K)]
    inv_z = 1.0 / zsum
    for g in range(G):
        rows = slice(g * SUBLANES, (g + 1) * SUBLANES)
        n = jnp.zeros((SUBLANES, tt), F32)
        for i in range(K - 1, -1, -1):
            n = jnp.where(s1[g] == v1[i], n_rep[i], n)
        n_ref[hd, rows, :] = n
        r1_ref[hd, rows, :] = jnp.exp(s1[g] - v1[0]) * inv_z
    for g in range(0, G, 2):
        rk, e2 = [], []
        for gg in (g, g + 1):
            rk.append(_count_above(v2, s2[gg]))
            e2.append(jnp.exp(s2[gg] - v2[0]))
        rows = slice(g * SUBLANES, (g + 2) * SUBLANES)
        rank_ref[hd, rows, :] = jnp.concatenate(rk, axis=0).astype(BF16)
        e2_ref[hd, rows, :] = jnp.concatenate(e2, axis=0).astype(BF16)


def _peer_topk(s1t, s2t, tt, hb):
    H, NK, T = s1t.shape
    sspec = pl.BlockSpec((hb, NK, tt), lambda i, h: (h, 0, i))
    return pl.pallas_call(
        _peer_topk_kernel,
        grid=(T // tt, H // hb),
        in_specs=[sspec, sspec],
        out_specs=[sspec, sspec, sspec, sspec],
        out_shape=[jax.ShapeDtypeStruct((H, NK, T), BF16), jax.ShapeDtypeStruct((H, NK, T), BF16),
                   jax.ShapeDtypeStruct((H, NK, T), F32), jax.ShapeDtypeStruct((H, NK, T), F32)],
        compiler_params=_cparams(("parallel", "parallel")),
    )(s1t, s2t)


def _peer_kernel(x_ref, ht_ref, htn_ref, rank_ref, e2_ref, n_ref, r1_ref, u0_ref, u1_ref, ua_ref, ub_ref,
                 va_ref, vb_ref, fw_ref, o_ref, act_a, act_b, p_a, p_b, acc_ref, *, eb, tt, final):
    s = pl.program_id(1)
    NK = PEER_NKEYS
    assert eb // NK == SUBLANES
    zero = jnp.zeros((), BF16)

    def build_p(blk, act_ref, p_ref, tc):
        a0 = pl.multiple_of(blk * SUBLANES, SUBLANES)
        ts = slice(tc * LANES, (tc + 1) * LANES)
        n8 = [n_ref[h, pl.ds(a0, SUBLANES), ts] for h in range(PEER_HEADS)]
        r8 = [r1_ref[h, pl.ds(a0, SUBLANES), ts] for h in range(PEER_HEADS)]
        for ai in range(SUBLANES):
            w = None
            for h in range(PEER_HEADS):
                n_row = n8[h][ai:ai + 1, :].astype(BF16)
                r_row = r8[h][ai:ai + 1, :].astype(BF16)
                term = jnp.where(rank_ref[h, :, ts] < n_row, e2_ref[h, :, ts] * r_row, zero)
                w = term if w is None else w + term
            act = act_ref[ai * NK:(ai + 1) * NK, ts]
            gelu = 0.5 * act * (1.0 + lax.erf(act * (2.0 ** -0.5)))
            p_ref[ai * NK:(ai + 1) * NK, ts] = w * gelu.astype(BF16)

    @pl.when(s == 0)
    def _():
        acc_ref[...] = jnp.zeros_like(acc_ref)

    @pl.when((s == 0) & (pl.program_id(0) == 0))
    def _():
        act_a[...] = _dot(u0_ref[...], ht_ref[...])
        act_b[...] = _dot(u1_ref[...], ht_ref[...])

    half = tt // 2
    per_half = half // LANES

    def stage(blk, act_ref, p_ref, v_ref, u_next_ref):
        for c in range(2):
            cs = slice(c * half, (c + 1) * half)
            for tc in range(c * per_half, (c + 1) * per_half):
                build_p(blk, act_ref, p_ref, tc)
            acc_ref[:, cs] += _dot(v_ref[...], p_ref[:, cs])
            act_ref[:, cs] = _dot(u_next_ref[...], htn_ref[:, cs])

    stage(2 * s, act_a, p_a, va_ref, ua_ref)
    stage(2 * s + 1, act_b, p_b, vb_ref, ub_ref)

    @pl.when(s == pl.num_programs(1) - 1)
    def _():
        y = x_ref[...] + acc_ref[...].T
        if final:
            ms = jnp.mean(y * y, axis=-1, keepdims=True)
            y = y * lax.rsqrt(ms + EPS) * fw_ref[...]
        o_ref[...] = y


def _peer(x, h2t, rank2, e2, n1, r1, u, vt, final_w, tt, eb, final):
    T, D = x.shape
    H, NK = PEER_HEADS, PEER_NKEYS
    ne = vt.shape[0]
    assert h2t.shape == (T // tt, D, tt) and vt.shape == (ne, D, eb) and u.shape == (ne * eb, D)
    sspec = pl.BlockSpec((H, NK, tt), lambda i, s: (0, 0, i))
    once = pl.Buffered(1)
    nt, ns = T // tt, ne // 2
    return pl.pallas_call(
        functools.partial(_peer_kernel, eb=eb, tt=tt, final=final),
        grid=(nt, ns),
        in_specs=[pl.BlockSpec((tt, D), lambda i, s: (i, 0)),
                  pl.BlockSpec((None, D, tt), lambda i, s: (i, 0, 0)),
                  pl.BlockSpec((None, D, tt), lambda i, s: (jnp.minimum(i + (s + 1) // ns, nt - 1), 0, 0)),
                  sspec, sspec, sspec, sspec,
                  pl.BlockSpec((eb, D), lambda i, s: (0, 0), pipeline_mode=once),
                  pl.BlockSpec((eb, D), lambda i, s: (1, 0), pipeline_mode=once),
                  pl.BlockSpec((eb, D), lambda i, s: ((2 * s + 2) % ne, 0)),
                  pl.BlockSpec((eb, D), lambda i, s: ((2 * s + 3) % ne, 0)),
                  pl.BlockSpec((None, D, eb), lambda i, s: (2 * s, 0, 0)),
                  pl.BlockSpec((None, D, eb), lambda i, s: (2 * s + 1, 0, 0)),
                  pl.BlockSpec((1, D), lambda i, s: (0, 0))],
        out_specs=pl.BlockSpec((tt, D), lambda i, s: (i, 0)),
        out_shape=jax.ShapeDtypeStruct((T, D), F32),
        scratch_shapes=[pltpu.VMEM((eb, tt), F32), pltpu.VMEM((eb, tt), F32),
                        pltpu.VMEM((eb, tt), BF16), pltpu.VMEM((eb, tt), BF16),
                        pltpu.VMEM((D, tt), F32)],
        compiler_params=_cparams(("arbitrary", "arbitrary")),
    )(x, h2t, h2t, rank2, e2, n1, r1, u, u, u, u, vt, vt, final_w.reshape(1, D))


def _layer(x, batch, seq, lb, p, final_w, final, tiles):
    proj = _inproj(x, p["norm1_w"], p["w_in"], tiles["in_tm"], tiles["in_tn"])
    y_sb = _sb_attention(proj, batch, seq, tiles["sb_blk"], tiles["sb_nsub"])
    y_hg = _hgrn2(proj, lb, p["hg_norm_w"], batch, seq, tiles["hg_ct"])
    y_ssd = _ssd(proj, p["ssd_conv_w"], p["ssd_conv_b"], p["ssd_dt_bias"], p["ssd_a_log"],
                 p["ssd_d"], p["ssd_norm_w"], batch, seq)
    x, h2 = _merge(y_sb, y_hg, y_ssd, proj, x, p["w_branch_sb"], p["w_branch_hg"],
                   p["w_branch_ssd"], p["w_out"], p["norm2_w"], tiles["mg_tm"])
    s1t, s2t, h2t = _peer_query(h2, p["peer_w_q"], p["peer_sub_keys"], tiles["pq_tm"])
    rank2, e2, n1, r1 = _peer_topk(s1t, s2t, tiles["tk_tt"], tiles["tk_hb"])
    return _peer(x, h2t, rank2, e2, n1, r1, p["peer_u"], p["peer_vt"], final_w,
                 tiles["pe_tt"], tiles["pe_eb"], final)


TILES = dict(in_tm=2048, in_tn=1408, sb_blk=128, sb_nsub=2, hg_ct=256, mg_tm=256, pq_tm=512,
             tk_tt=256, tk_hb=2, pe_tt=512, pe_eb=1024)


def _prep_w_in(w):
    c = np.cumsum([0, 512, 512, 512, 512, 512, 512, 512, 512, 768, 8, 1024, 1024, 1024])
    dt_rep = jnp.repeat(w[:, c[9]:c[10]], SSD_HEAD_DIM, axis=1)
    return jnp.concatenate([w[:, c[3]:c[7]], w[:, c[7]:c[8]], dt_rep, w[:, c[10]:c[13]],
                            w[:, c[8]:c[9]], w[:, c[0]:c[3]]], axis=1).astype(BF16)


def _trunk(x, norm1_w, w_in, hg_lb_logits, hg_norm_w, ssd_conv_w, ssd_conv_b, ssd_dt_bias,
           ssd_a_log, ssd_d, ssd_norm_w, w_branch_sb, w_branch_hg, w_branch_ssd, w_out,
           norm2_w, peer_w_q, peer_sub_keys, peer_u, peer_v, final_norm_w, tiles):
    batch, seq, d = x.shape
    depth = w_in.shape[0]
    gamma = jax.nn.softmax(hg_lb_logits.astype(F32), axis=0)
    lower_bounds = jnp.cumsum(gamma, axis=0) - gamma[0]
    xt = x.reshape(batch * seq, d)
    for l in range(depth):
        p = dict(
            norm1_w=norm1_w[l], w_in=_prep_w_in(w_in[l]), hg_norm_w=hg_norm_w[l],
            ssd_conv_w=ssd_conv_w[l], ssd_conv_b=ssd_conv_b[l], ssd_dt_bias=ssd_dt_bias[l],
            ssd_a_log=ssd_a_log[l], ssd_d=ssd_d[l], ssd_norm_w=ssd_norm_w[l],
            w_branch_sb=w_branch_sb[l].astype(BF16), w_branch_hg=w_branch_hg[l].astype(BF16),
            w_branch_ssd=w_branch_ssd[l].astype(BF16), w_out=w_out[l].astype(BF16),
            norm2_w=norm2_w[l], peer_w_q=peer_w_q[l].astype(BF16),
            peer_sub_keys=peer_sub_keys[l].reshape(2 * PEER_HEADS, PEER_NKEYS, -1).astype(BF16),
            peer_u=peer_u[l].astype(BF16),
            peer_vt=peer_v[l].reshape(-1, tiles["pe_eb"], d).transpose(0, 2, 1).astype(BF16))
        xt = _layer(xt, batch, seq, lower_bounds[l], p, final_norm_w, l == depth - 1, tiles)
    return xt.reshape(batch, seq, d)


def kernel(x, norm1_w, w_in, hg_lb_logits, hg_norm_w, ssd_conv_w, ssd_conv_b, ssd_dt_bias, ssd_a_log, ssd_d, ssd_norm_w, w_branch_sb, w_branch_hg, w_branch_ssd, w_out, norm2_w, peer_w_q, peer_sub_keys, peer_u, peer_v, final_norm_w):
    return _trunk(x, norm1_w, w_in, hg_lb_logits, hg_norm_w, ssd_conv_w, ssd_conv_b, ssd_dt_bias,
                  ssd_a_log, ssd_d, ssd_norm_w, w_branch_sb, w_branch_hg, w_branch_ssd, w_out,
                  norm2_w, peer_w_q, peer_sub_keys, peer_u, peer_v, final_norm_w, TILES)
```
